```python
import math
import jax, jax.numpy as jnp
from jax import lax
import numpy as np

D_MODEL = 2048
BATCH = 1
SEQ = 8192
DEPTH = 2
DEC_BATCH = 128
DEC_SEQ = 8
PAST_LEN = 8192
PAGE_SIZE = 128

N_Q_HEADS = 16
N_KV_HEADS = 2
KV_GROUP = N_Q_HEADS // N_KV_HEADS
HEAD_DIM = 64
ATTN_WIDTH = N_Q_HEADS * HEAD_DIM
KV_WIDTH = N_KV_HEADS * HEAD_DIM
WINDOW = 128
GLA_HEADS = 4
GLA_DV = D_MODEL // 8
GLA_DK = GLA_DV // 2
GLA_KEY_WIDTH = GLA_HEADS * GLA_DK
GLA_VAL_WIDTH = GLA_HEADS * GLA_DV
GLA_LOWRANK = 16
GLA_TAU = 16.0
GLA_CHUNK = 64
N_GROUPS = 4
EXPERTS_PER_GROUP = 8
N_EXPERTS = N_GROUPS * EXPERTS_PER_GROUP
TOP_K_INNER = 2
EXPERT_FF = D_MODEL // 4
MOE_TOKEN_BLOCK = 1024
RMS_EPS = 1e-5
IN_WIDTHS = (ATTN_WIDTH, KV_WIDTH, KV_WIDTH, GLA_KEY_WIDTH, GLA_KEY_WIDTH, GLA_VAL_WIDTH,
             GLA_LOWRANK, GLA_VAL_WIDTH, D_MODEL, D_MODEL)
N_IN = sum(IN_WIDTHS)

kernel_name = "hybrid_swa_sink_gla_hiermoe_step"

F32 = jnp.float32


def rmsnorm(x, g):
    xf = x.astype(F32)
    y = xf * lax.rsqrt(jnp.mean(xf * xf, axis=-1, keepdims=True) + RMS_EPS)
    return (y * g.astype(F32)).astype(x.dtype)


def _sink_attend(s, mask, sink, v, eq):
    s = jnp.where(mask, s, -jnp.inf)
    m = jnp.maximum(jnp.max(s, axis=-1), sink)
    p = jnp.exp(s - m[..., None])
    den = jnp.sum(p, axis=-1) + jnp.exp(sink - m)
    p = p / den[..., None]
    return jnp.einsum(eq, p.astype(v.dtype), v)


def swa_prompt(q, k, v, sinks):
    B, T = q.shape[:2]
    nb = T // WINDOW
    qb = q.reshape(B, nb, WINDOW, N_KV_HEADS, KV_GROUP, HEAD_DIM)
    kb = k.reshape(B, nb, WINDOW, N_KV_HEADS, HEAD_DIM)
    vb = v.reshape(B, nb, WINDOW, N_KV_HEADS, HEAD_DIM)
    pad = jnp.zeros_like(kb[:, :1])
    kk = jnp.concatenate([jnp.concatenate([pad, kb[:, :-1]], axis=1), kb], axis=2)
    vv = jnp.concatenate([jnp.concatenate([pad, vb[:, :-1]], axis=1), vb], axis=2)
    s = jnp.einsum('bnqhgd,bnkhd->bnhgqk', qb, kk, preferred_element_type=F32) * (HEAD_DIM ** -0.5)
    blk = jnp.arange(nb)[:, None] * WINDOW
    qpos = blk + jnp.arange(WINDOW)[None]
    kpos = blk - WINDOW + jnp.arange(2 * WINDOW)[None]
    diff = qpos[:, :, None] - kpos[:, None, :]
    mask = (diff >= 0) & (diff < WINDOW) & (kpos[:, None, :] >= 0)
    o = _sink_attend(s, mask[None, :, None, None], sinks, vv, 'bnhgqk,bnkhd->bnqhgd')
    w_keep = min(WINDOW, T)
    return o.reshape(B, T, ATTN_WIDTH), k[:, T - w_keep:], v[:, T - w_keep:]


def swa_sample(q, k, v, k_buf, v_buf, sinks):
    B, L = q.shape[:2]
    wb = k_buf.shape[1]
    kk = jnp.concatenate([k_buf, k], axis=1)
    vv = jnp.concatenate([v_buf, v], axis=1)
    qg = q.reshape(B, L, N_KV_HEADS, KV_GROUP, HEAD_DIM)
    s = jnp.einsum('bqhgd,bkhd->bhgqk', qg, kk, preferred_element_type=F32) * (HEAD_DIM ** -0.5)
    qrel = wb + jnp.arange(L)
    krel = jnp.arange(wb + L)
    diff = qrel[:, None] - krel[None, :]
    mask = (diff >= 0) & (diff < WINDOW)
    o = _sink_attend(s, mask[None, None, None], sinks, vv, 'bhgqk,bkhd->bqhgd')
    return o.reshape(B, L, ATTN_WIDTH), kk[:, L:], vv[:, L:]


def gla_recurrence(q, k, v, log_a, S0, chunk):
    B, T, H, DK = q.shape
    DV = v.shape[-1]
    nc = T // chunk

    def split(a):
        return a.astype(F32).reshape(B, nc, chunk, H, a.shape[-1]).transpose(1, 0, 3, 2, 4)

    causal = jnp.tril(jnp.ones((chunk, chunk), bool))[:, :, None]

    def step(S, inp):
        qc, kc, vc, gc = inp
        b = jnp.cumsum(gc, axis=2)
        inter = jnp.einsum('bhik,bhkv->bhiv', qc * jnp.exp(b), S)
        diff = b[:, :, :, None, :] - b[:, :, None, :, :]
        decay = jnp.exp(jnp.where(causal, diff, -jnp.inf))
        attn = jnp.einsum('bhik,bhjk,bhijk->bhij', qc, kc, decay)
        intra = jnp.einsum('bhij,bhjv->bhiv', attn, vc)
        b_last = b[:, :, -1:, :]
        S = jnp.exp(b_last[:, :, 0, :, None]) * S + jnp.einsum('bhjk,bhjv->bhkv', kc * jnp.exp(b_last - b), vc)
        return S, inter + intra

    S, o = lax.scan(step, S0.astype(F32), (split(q), split(k), split(v), split(log_a)))
    o = o.transpose(1, 0, 3, 2, 4).reshape(B, T, H, DV)
    return o, S


def mixer_block(x, ln1_g, w_in, w_alpha, b_alpha, gla_norm_g, sinks, w_br_attn, w_br_gla, w_out,
                k_buf, v_buf, gla_s0):
    B, T = x.shape[:2]
    xn = rmsnorm(x, ln1_g)
    z = xn @ w_in
    q_a, k_a, v_a, q_g, k_g, v_g, a_lr, r_g, gate_a, gate_g = jnp.split(
        z, np.cumsum(IN_WIDTHS)[:-1].tolist(), axis=-1)
    q_a = q_a.reshape(B, T, N_Q_HEADS, HEAD_DIM)
    k_a = k_a.reshape(B, T, N_KV_HEADS, HEAD_DIM)
    v_a = v_a.reshape(B, T, N_KV_HEADS, HEAD_DIM)
    sink = sinks.astype(F32).reshape(N_KV_HEADS, KV_GROUP)[:, :, None]
    if k_buf is None:
        o_a, k_new, v_new = swa_prompt(q_a, k_a, v_a, sink)
        s0 = jnp.zeros((B, GLA_HEADS, GLA_DK, GLA_DV), F32)
    else:
        o_a, k_new, v_new = swa_sample(q_a, k_a, v_a, k_buf, v_buf, sink)
        s0 = gla_s0
    log_a = jax.nn.log_sigmoid((a_lr @ w_alpha + b_alpha).astype(F32)) / GLA_TAU
    log_a = log_a.reshape(B, T, GLA_HEADS, GLA_DK)
    qg = q_g.reshape(B, T, GLA_HEADS, GLA_DK) * (GLA_DK ** -0.5)
    kg = k_g.reshape(B, T, GLA_HEADS, GLA_DK)
    vg = v_g.reshape(B, T, GLA_HEADS, GLA_DV)
    o_g, s_new = gla_recurrence(qg, kg, vg, log_a, s0, math.gcd(T, GLA_CHUNK))
    o_g = rmsnorm(o_g.astype(x.dtype), gla_norm_g) * jax.nn.silu(r_g).reshape(B, T, GLA_HEADS, GLA_DV)
    o_g = o_g.reshape(B, T, GLA_VAL_WIDTH)
    merged = jax.nn.sigmoid(gate_a) * (o_a @ w_br_attn) + jax.nn.sigmoid(gate_g) * (o_g @ w_br_gla)
    return x + merged @ w_out, k_new, v_new, s_new.astype(x.dtype)


def hier_moe(xn, w_rg, b_rg, w_re, b_re, w_gate, w_up, w_down):
    B, T, D = xn.shape
    n = B * T
    blk = math.gcd(n, MOE_TOKEN_BLOCK)

    def block_fn(xb):
        gp = jax.nn.softmax((xb @ w_rg + b_rg).astype(F32), axis=-1)
        g_idx = jnp.argmax(gp, axis=-1)
        p_g = jnp.take_along_axis(gp, g_idx[:, None], axis=-1)
        el = (xb @ w_re + b_re).astype(F32).reshape(-1, N_GROUPS, EXPERTS_PER_GROUP)
        el = jnp.take_along_axis(el, g_idx[:, None, None], axis=1)[:, 0]
        top_v, top_i = lax.top_k(jax.nn.softmax(el, axis=-1), TOP_K_INNER)
        w = p_g * top_v / jnp.sum(top_v, axis=-1, keepdims=True)
        e_id = g_idx[:, None] * EXPERTS_PER_GROUP + top_i
        combine = jnp.sum(jax.nn.one_hot(e_id, N_EXPERTS, dtype=F32) * w[..., None], axis=1)
        h = jax.nn.silu(jnp.einsum('td,edf->tef', xb, w_gate)) * jnp.einsum('td,edf->tef', xb, w_up)
        h = h * combine[:, :, None].astype(h.dtype)
        return jnp.einsum('tef,efd->td', h, w_down)

    out = lax.map(block_fn, xn.reshape(n // blk, blk, D))
    return out.reshape(B, T, D)


def trunk(x, swa_k, swa_v, gla_s, weights):
    (ln1_g, w_in, w_alpha, b_alpha, gla_norm_g, attn_sinks, w_br_attn, w_br_gla, w_out, ln2_g,
     w_router_group, b_router_group, w_router_expert, b_router_expert,
     w_exp_gate, w_exp_up, w_exp_down, final_norm_g) = weights
    ks, vs, ss = [], [], []
    for l in range(DEPTH):
        kb = None if swa_k is None else swa_k[l]
        vb = None if swa_v is None else swa_v[l]
        sb = None if gla_s is None else gla_s[l]
        x, k_new, v_new, s_new = mixer_block(
            x, ln1_g[l], w_in[l], w_alpha[l], b_alpha[l], gla_norm_g[l], attn_sinks[l],
            w_br_attn[l], w_br_gla[l], w_out[l], kb, vb, sb)
        x = x + hier_moe(rmsnorm(x, ln2_g[l]), w_router_group[l], b_router_group[l],
                         w_router_expert[l], b_router_expert[l],
                         w_exp_gate[l], w_exp_up[l], w_exp_down[l])
        ks.append(k_new)
        vs.append(v_new)
        ss.append(s_new)
    return rmsnorm(x, final_norm_g), jnp.stack(ks), jnp.stack(vs), jnp.stack(ss)


def setup_inputs(seed: int = 0) -> dict:
    key = jax.random.key(seed)
    ks = jax.random.split(key, 24)

    def nrm(k, shape, scale):
        return jax.random.normal(k, shape, F32) * scale

    def gain(k, shape):
        return 1.0 + 0.02 * jax.random.normal(k, shape, F32)

    wb = min(WINDOW, PAST_LEN)
    return {
        "x_prompt": nrm(ks[0], (BATCH, SEQ, D_MODEL), 1.0),
        "x_sample": nrm(ks[1], (DEC_BATCH, DEC_SEQ, D_MODEL), 1.0),
        "cache_swa_k": nrm(ks[2], (DEPTH, DEC_BATCH, wb, N_KV_HEADS, HEAD_DIM), 1.0),
        "cache_swa_v": nrm(ks[3], (DEPTH, DEC_BATCH, wb, N_KV_HEADS, HEAD_DIM), 1.0),
        "state_gla": nrm(ks[4], (DEPTH, DEC_BATCH, GLA_HEADS, GLA_DK, GLA_DV), 1.0),
        "ln1_g": gain(ks[5], (DEPTH, D_MODEL)),
        "w_in": nrm(ks[6], (DEPTH, D_MODEL, N_IN), D_MODEL ** -0.5),
        "w_alpha": nrm(ks[7], (DEPTH, GLA_LOWRANK, GLA_KEY_WIDTH), GLA_LOWRANK ** -0.5),
        "b_alpha": nrm(ks[8], (DEPTH, GLA_KEY_WIDTH), 0.1),
        "gla_norm_g": gain(ks[9], (DEPTH, GLA_DV)),
        "attn_sinks": nrm(ks[10], (DEPTH, N_Q_HEADS), 0.5),
        "w_br_attn": nrm(ks[11], (DEPTH, ATTN_WIDTH, D_MODEL), ATTN_WIDTH ** -0.5),
        "w_br_gla": nrm(ks[12], (DEPTH, GLA_VAL_WIDTH, D_MODEL), GLA_VAL_WIDTH ** -0.5),
        "w_out": nrm(ks[13], (DEPTH, D_MODEL, D_MODEL), D_MODEL ** -0.5),
        "ln2_g": gain(ks[14], (DEPTH, D_MODEL)),
        "w_router_group": nrm(ks[15], (DEPTH, D_MODEL, N_GROUPS), D_MODEL ** -0.5),
        "b_router_group": nrm(ks[16], (DEPTH, N_GROUPS), 0.01),
        "w_router_expert": nrm(ks[17], (DEPTH, D_MODEL, N_EXPERTS), D_MODEL ** -0.5),
        "b_router_expert": nrm(ks[18], (DEPTH, N_EXPERTS), 0.01),
        "w_exp_gate": nrm(ks[19], (DEPTH, N_EXPERTS, D_MODEL, EXPERT_FF), D_MODEL ** -0.5),
        "w_exp_up": nrm(ks[20], (DEPTH, N_EXPERTS, D_MODEL, EXPERT_FF), D_MODEL ** -0.5),
        "w_exp_down": nrm(ks[21], (DEPTH, N_EXPERTS, EXPERT_FF, D_MODEL), EXPERT_FF ** -0.5),
        "final_norm_g": gain(ks[22], (D_MODEL,)),
    }


def reference(x_prompt, x_sample, cache_swa_k, cache_swa_v, state_gla,
              ln1_g, w_in, w_alpha, b_alpha, gla_norm_g, attn_sinks, w_br_attn, w_br_gla, w_out,
              ln2_g, w_router_group, b_router_group, w_router_expert, b_router_expert,
              w_exp_gate, w_exp_up, w_exp_down, final_norm_g):
    weights = (ln1_g, w_in, w_alpha, b_alpha, gla_norm_g, attn_sinks, w_br_attn, w_br_gla, w_out,
               ln2_g, w_router_group, b_router_group, w_router_expert, b_router_expert,
               w_exp_gate, w_exp_up, w_exp_down, final_norm_g)
    y_prompt, pk, pv, ps = trunk(x_prompt, None, None, None, weights)
    y_sample, sk, sv, ss = trunk(x_sample, cache_swa_k, cache_swa_v, state_gla, weights)
    return (y_prompt, y_sample, pk, pv, ps, sk, sv, ss)
```

```python
import functools

import jax
import jax.numpy as jnp
from jax import lax
from jax.experimental import pallas as pl
from jax.experimental.pallas import tpu as pltpu

F32 = jnp.float32
BF16 = jnp.bfloat16
I32 = jnp.int32

D_MODEL = 2048
DEPTH = 2
N_Q_HEADS = 16
N_KV_HEADS = 2
KV_GROUP = N_Q_HEADS // N_KV_HEADS
HEAD_DIM = 64
ATTN_WIDTH = N_Q_HEADS * HEAD_DIM
KV_WIDTH = N_KV_HEADS * HEAD_DIM
WINDOW = 128
GLA_HEADS = 4
GLA_DV = 256
GLA_DK = 128
GLA_KEY_WIDTH = GLA_HEADS * GLA_DK
GLA_VAL_WIDTH = GLA_HEADS * GLA_DV
GLA_LOWRANK = 16
GLA_TAU = 16.0
N_GROUPS = 4
EXPERTS_PER_GROUP = 8
N_EXPERTS = N_GROUPS * EXPERTS_PER_GROUP
EXPERT_FF = 512
RMS_EPS = 1e-5

LANES = 128
SUBLANES = 8
VMEM_LIMIT = 56 * 1024 * 1024

COL_GATE_A = 0
COL_GATE_G = 2048
COL_Q_A = 4096
COL_V_G = 5120
COL_R_G = 6144
COL_Q_G = 7168
COL_K_G = 7680
COL_K_A = 8192
COL_V_A = 8320
COL_A_LR = 8448
N_PROJ = 8704

_O_QA, _O_KA, _O_VA, _O_QG, _O_KG, _O_VG, _O_ALR, _O_RG, _O_GA, _O_GG, _O_END = (
    0, 1024, 1152, 1280, 1792, 2304, 3328, 3344, 4368, 6416, 8464)

NEG_INF = float("-inf")


def _cparams(sem):
    return pltpu.CompilerParams(dimension_semantics=sem, vmem_limit_bytes=VMEM_LIMIT)


def _permute_w_in(w_in):
    s = lambda a, b: w_in[:, :, a:b]
    pad = jnp.zeros(w_in.shape[:2] + (N_PROJ - COL_A_LR - GLA_LOWRANK,), w_in.dtype)
    parts = [s(_O_GA, _O_GG), s(_O_GG, _O_END), s(_O_QA, _O_KA), s(_O_VG, _O_ALR),
             s(_O_RG, _O_GA), s(_O_QG, _O_KG), s(_O_KG, _O_VG), s(_O_KA, _O_VA),
             s(_O_VA, _O_QG), s(_O_ALR, _O_RG), pad]
    return jnp.concatenate(parts, axis=-1).astype(BF16)


INPROJ_TM = 1536
INPROJ_TN = 512
NORM_ROWS = 256


def _rmsnorm_rows(x, g):
    ms = jnp.mean(x * x, axis=-1, keepdims=True)
    return x * lax.rsqrt(ms + RMS_EPS) * g


def _inproj_kernel(x_ref, g_ref, w_ref, z_ref, xn_ref):
    @pl.when(pl.program_id(1) == 0)
    def _():
        g = g_ref[...]

        def body(r, c):
            rows = pl.ds(pl.multiple_of(r * NORM_ROWS, NORM_ROWS), NORM_ROWS)
            xn_ref[rows, :] = _rmsnorm_rows(x_ref[rows, :], g).astype(BF16)
            return c

        lax.fori_loop(0, x_ref.shape[0] // NORM_ROWS, body, 0)

    z_ref[...] = jnp.dot(xn_ref[...], w_ref[...], preferred_element_type=F32)


def _inproj(x, g, w, tm=INPROJ_TM, tn=INPROJ_TN):
    n, d = x.shape
    npj = w.shape[1]
    tm = min(tm, n)
    return pl.pallas_call(
        _inproj_kernel,
        grid=(n // tm, npj // tn),
        in_specs=[pl.BlockSpec((tm, d), lambda i, j: (i, 0)),
                  pl.BlockSpec((1, d), lambda i, j: (0, 0)),
                  pl.BlockSpec((d, tn), lambda i, j: (0, j))],
        out_specs=pl.BlockSpec((tm, tn), lambda i, j: (i, j)),
        out_shape=jax.ShapeDtypeStruct((n, npj), F32),
        scratch_shapes=[pltpu.VMEM((tm, d), BF16)],
        compiler_params=_cparams(("arbitrary", "arbitrary")),
        name="inproj",
    )(x, g, w)


def _sink_softmax_pv(s, mask, sink, v_bf):
    s = jnp.where(mask, s, NEG_INF)
    m = jnp.maximum(jnp.max(s, axis=-1, keepdims=True), sink)
    p = jnp.exp(s - m)
    den = jnp.sum(p, axis=-1, keepdims=True) + jnp.exp(sink - m)
    o = jnp.dot(p.astype(BF16), v_bf, preferred_element_type=F32)
    return o / den


def _swa_prompt_kernel(sink_ref, q_ref, kc_ref, kp_ref, vc_ref, vp_ref, o_ref):
    n = pl.program_id(0)
    w = WINDOW
    q = (q_ref[...] * (HEAD_DIM ** -0.5)).astype(BF16)
    kk = jnp.concatenate([kp_ref[...], kc_ref[...]], axis=0).astype(BF16)
    vv = jnp.concatenate([vp_ref[...], vc_ref[...]], axis=0).astype(BF16)
    iq = lax.broadcasted_iota(I32, (w, 2 * w), 0)
    ik = lax.broadcasted_iota(I32, (w, 2 * w), 1)
    diff = iq + w - ik
    mask = (diff >= 0) & (diff < w) & ((ik >= w) | (n > 0))
    outs = []
    for h in range(N_KV_HEADS):
        kh = kk[:, h * HEAD_DIM:(h + 1) * HEAD_DIM]
        vh = vv[:, h * HEAD_DIM:(h + 1) * HEAD_DIM]
        for g in range(KV_GROUP):
            hq = h * KV_GROUP + g
            qh = q[:, hq * HEAD_DIM:(hq + 1) * HEAD_DIM]
            s = lax.dot_general(qh, kh, (((1,), (1,)), ((), ())), preferred_element_type=F32)
            outs.append(_sink_softmax_pv(s, mask, sink_ref[hq], vh))
    o_ref[...] = jnp.concatenate(outs, axis=1).astype(o_ref.dtype)


def _swa_prompt(z, sinks, t):
    w = WINDOW
    nb = t // w
    cq, ck, cv = COL_Q_A // ATTN_WIDTH, COL_K_A // KV_WIDTH, COL_V_A // KV_WIDTH
    grid_spec = pltpu.PrefetchScalarGridSpec(
        num_scalar_prefetch=1,
        grid=(nb,),
        in_specs=[pl.BlockSpec((w, ATTN_WIDTH), lambda n, s: (n, cq)),
                  pl.BlockSpec((w, KV_WIDTH), lambda n, s: (n, ck)),
                  pl.BlockSpec((w, KV_WIDTH), lambda n, s: (jnp.maximum(n - 1, 0), ck)),
                  pl.BlockSpec((w, KV_WIDTH), lambda n, s: (n, cv)),
                  pl.BlockSpec((w, KV_WIDTH), lambda n, s: (jnp.maximum(n - 1, 0), cv))],
        out_specs=pl.BlockSpec((w, ATTN_WIDTH), lambda n, s: (n, 0)),
    )
    return pl.pallas_call(
        _swa_prompt_kernel,
        grid_spec=grid_spec,
        out_shape=jax.ShapeDtypeStruct((t, ATTN_WIDTH), BF16),
        compiler_params=_cparams(("arbitrary",)),
        name="swa_prompt",
    )(sinks, z, z, z, z, z)


SWA_S_BB = 8


def _swa_sample_kernel(sink_ref, q_ref, kn_ref, vn_ref, kc_ref, vc_ref,
                       o_ref, ko_ref, vo_ref, *, l):
    w = WINDOW
    bb = kc_ref.shape[0]
    kpad = 2 * w - w - l
    rows = KV_GROUP * l
    ir = lax.broadcasted_iota(I32, (rows, 2 * w), 0)
    ik = lax.broadcasted_iota(I32, (rows, 2 * w), 1)
    lq = ir & (l - 1)
    diff = w + lq - ik
    mask = (diff >= 0) & (diff < w)
    gi = lax.broadcasted_iota(I32, (rows, 1), 0) >> (l.bit_length() - 1)
    zpad = jnp.zeros((kpad, KV_WIDTH), F32)
    for b in range(bb):
        q = (q_ref[b * l:(b + 1) * l, :] * (HEAD_DIM ** -0.5)).astype(BF16)
        kn = kn_ref[b * l:(b + 1) * l, :]
        vn = vn_ref[b * l:(b + 1) * l, :]
        kc = kc_ref[b]
        vc = vc_ref[b]
        ko_ref[b] = jnp.concatenate([kc[l:, :], kn], axis=0)
        vo_ref[b] = jnp.concatenate([vc[l:, :], vn], axis=0)
        kk = jnp.concatenate([kc, kn, zpad], axis=0).astype(BF16)
        vv = jnp.concatenate([vc, vn, zpad], axis=0).astype(BF16)
        pieces = []
        for h in range(N_KV_HEADS):
            kh = kk[:, h * HEAD_DIM:(h + 1) * HEAD_DIM]
            vh = vv[:, h * HEAD_DIM:(h + 1) * HEAD_DIM]
            qh = jnp.concatenate(
                [q[:, (h * KV_GROUP + g) * HEAD_DIM:(h * KV_GROUP + g + 1) * HEAD_DIM]
                 for g in range(KV_GROUP)], axis=0)
            sink = jnp.zeros((rows, 1), F32)
            for g in range(KV_GROUP):
                sink = jnp.where(gi == g, sink_ref[h * KV_GROUP + g], sink)
            s = lax.dot_general(qh, kh, (((1,), (1,)), ((), ())), preferred_element_type=F32)
            o = _sink_softmax_pv(s, mask, sink, vh)
            pieces.extend(o[g * l:(g + 1) * l, :] for g in range(KV_GROUP))
        o_ref[b * l:(b + 1) * l, :] = jnp.concatenate(pieces, axis=1).astype(o_ref.dtype)


def _swa_sample(z, sinks, k_cache, v_cache, row0, nseq, l):
    assert l & (l - 1) == 0
    w = WINDOW
    bb = SWA_S_BB
    rb = bb * l
    r0 = row0 // rb
    cq = COL_Q_A // ATTN_WIDTH
    ck, cv = COL_K_A // KV_WIDTH, COL_V_A // KV_WIDTH
    grid_spec = pltpu.PrefetchScalarGridSpec(
        num_scalar_prefetch=1,
        grid=(nseq // bb,),
        in_specs=[pl.BlockSpec((rb, ATTN_WIDTH), lambda i, s: (r0 + i, cq)),
                  pl.BlockSpec((rb, KV_WIDTH), lambda i, s: (r0 + i, ck)),
                  pl.BlockSpec((rb, KV_WIDTH), lambda i, s: (r0 + i, cv)),
                  pl.BlockSpec((bb, w, KV_WIDTH), lambda i, s: (i, 0, 0)),
                  pl.BlockSpec((bb, w, KV_WIDTH), lambda i, s: (i, 0, 0))],
        out_specs=[pl.BlockSpec((rb, ATTN_WIDTH), lambda i, s: (i, 0)),
                   pl.BlockSpec((bb, w, KV_WIDTH), lambda i, s: (i, 0, 0)),
                   pl.BlockSpec((bb, w, KV_WIDTH), lambda i, s: (i, 0, 0))],
    )
    return pl.pallas_call(
        functools.partial(_swa_sample_kernel, l=l),
        grid_spec=grid_spec,
        out_shape=[jax.ShapeDtypeStruct((nseq * l, ATTN_WIDTH), BF16),
                   jax.ShapeDtypeStruct((nseq, w, KV_WIDTH), F32),
                   jax.ShapeDtypeStruct((nseq, w, KV_WIDTH), F32)],
        compiler_params=_cparams(("arbitrary",)),
        name="swa_sample",
    )(sinks, z, z, z, k_cache, v_cache)


GLA_SUB = SUBLANES


def _split3(x):
    hi = x.astype(BF16)
    r1 = x - hi.astype(F32)
    mid = r1.astype(BF16)
    lo = (r1 - mid.astype(F32)).astype(BF16)
    return hi, mid, lo


def _gla_kernel(q_ref, k_ref, v_ref, a_ref, r_ref, wa_ref, ba_ref, ng_ref, s0_ref,
                o_ref, so_ref, s_scr, *, c, nseq):
    ci = pl.program_id(1)
    nsb = c // GLA_SUB
    scale = GLA_DK ** -0.5

    @pl.when(ci == 0)
    def _():
        s_scr[...] = s0_ref[...]

    rb = nseq * c
    row = lax.broadcasted_iota(I32, (c, 1), 0)
    rsub = row & (GLA_SUB - 1)
    ri = lax.broadcasted_iota(I32, (rb, rb), 0)
    rj = lax.broadcasted_iota(I32, (rb, rb), 1)
    tri = ((rj <= ri) & (rj >= ri - (ri & (c - 1)))).astype(BF16)
    row_l = lax.broadcasted_iota(I32, (LANES, GLA_DK), 0)

    a_lr = a_ref[...].astype(BF16)
    pre = jnp.dot(a_lr, wa_ref[...].astype(BF16), preferred_element_type=F32) + ba_ref[...]
    log_a = jax.nn.log_sigmoid(pre) / GLA_TAU
    g_hi, g_mid, g_lo = _split3(log_a)
    b_full = (jnp.dot(tri, g_hi, preferred_element_type=F32)
              + jnp.dot(tri, g_mid, preferred_element_type=F32)
              + jnp.dot(tri, g_lo, preferred_element_type=F32))

    for sq in range(nseq):
        rows = slice(sq * c, (sq + 1) * c)
        b_all = b_full[rows, :]
        outs = []
        for h in range(GLA_HEADS):
            ks = slice(h * GLA_DK, (h + 1) * GLA_DK)
            vs = slice(h * GLA_DV, (h + 1) * GLA_DV)
            q = q_ref[rows, ks] * scale
            k = k_ref[rows, ks]
            v = v_ref[rows, vs]
            b = b_all[:, ks]
            s_prev = s_scr[sq, h]
            v_bf = v.astype(BF16)

            o = jnp.dot((q * jnp.exp(b)).astype(BF16), s_prev.astype(BF16),
                        preferred_element_type=F32)

            if nsb > 1:
                blocks = [jnp.zeros((GLA_SUB, c), F32)]
                for i in range(1, nsb):
                    r0 = i * GLA_SUB
                    c_i = b[r0:r0 + 1, :]
                    qt = q[r0:r0 + GLA_SUB, :] * jnp.exp(b[r0:r0 + GLA_SUB, :] - c_i)
                    kt = k * jnp.exp(jnp.where(row < r0, c_i - b, NEG_INF))
                    blocks.append(lax.dot_general(
                        qt.astype(BF16), kt.astype(BF16), (((1,), (1,)), ((), ())),
                        preferred_element_type=F32))
                attn = jnp.concatenate(blocks, axis=0)
                o = o + jnp.dot(attn.astype(BF16), v_bf, preferred_element_type=F32)

            for d in range(GLA_SUB):
                if d == 0:
                    wgt = jnp.sum(q * k, axis=-1, keepdims=True)
                    vr = v
                else:
                    kr = pltpu.roll(k, d, 0)
                    br = pltpu.roll(b, d, 0)
                    dec = jnp.exp(jnp.where(rsub >= d, b - br, NEG_INF))
                    wgt = jnp.sum(q * kr * dec, axis=-1, keepdims=True)
                    vr = pltpu.roll(v, d, 0)
                o = o + wgt * vr

            b_last = b[c - 1:c, :]
            kd = k * jnp.exp(b_last - b)
            e_last = jnp.exp(b_last)
            zrows = jnp.zeros((LANES - c, GLA_DK), F32)
            y = jnp.where(row_l == c, jnp.broadcast_to(e_last, (LANES, GLA_DK)),
                          jnp.concatenate([kd, zrows], axis=0))
            yt = y.T
            v_pad = jnp.concatenate([v_bf, jnp.zeros((LANES - c, GLA_DV), BF16)], axis=0)
            upd = jnp.dot(yt.astype(BF16), v_pad, preferred_element_type=F32)
            s_scr[sq, h] = yt[:, c:c + 1] * s_prev + upd

            r = r_ref[rows, vs]
            outs.append(_rmsnorm_rows(o, ng_ref[...]) * (r * jax.nn.sigmoid(r)))
        o_ref[rows, :] = jnp.concatenate(outs, axis=1).astype(o_ref.dtype)

    @pl.when(ci == pl.num_programs(1) - 1)
    def _():
        so_ref[...] = s_scr[...]


def _gla(z, w_alpha_p, b_alpha, norm_g, s0, row0, nbatch, t, c, nseq):
    nchunk = t // c
    assert nseq == 1 or nchunk == 1
    assert c & (c - 1) == 0 and c % GLA_SUB == 0 and c < LANES
    rb = nseq * c
    r0 = row0 // rb
    rmap = lambda j: (lambda bi, ci: (r0 + bi * nchunk + ci, j))
    in_specs = [pl.BlockSpec((rb, GLA_KEY_WIDTH), rmap(COL_Q_G // GLA_KEY_WIDTH)),
                pl.BlockSpec((rb, GLA_KEY_WIDTH), rmap(COL_K_G // GLA_KEY_WIDTH)),
                pl.BlockSpec((rb, GLA_VAL_WIDTH), rmap(COL_V_G // GLA_VAL_WIDTH)),
                pl.BlockSpec((rb, LANES), rmap(COL_A_LR // LANES)),
                pl.BlockSpec((rb, GLA_VAL_WIDTH), rmap(COL_R_G // GLA_VAL_WIDTH)),
                pl.BlockSpec((LANES, GLA_KEY_WIDTH), lambda bi, ci: (0, 0)),
                pl.BlockSpec((1, GLA_KEY_WIDTH), lambda bi, ci: (0, 0)),
                pl.BlockSpec((1, GLA_DV), lambda bi, ci: (0, 0)),
                pl.BlockSpec((nseq, GLA_HEADS, GLA_DK, GLA_DV), lambda bi, ci: (bi, 0, 0, 0))]
    out_specs = [pl.BlockSpec((rb, GLA_VAL_WIDTH), lambda bi, ci: (bi * nchunk + ci, 0)),
                 pl.BlockSpec((nseq, GLA_HEADS, GLA_DK, GLA_DV), lambda bi, ci: (bi, 0, 0, 0))]
    return pl.pallas_call(
        functools.partial(_gla_kernel, c=c, nseq=nseq),
        grid=(nbatch // nseq, nchunk),
        in_specs=in_specs,
        out_specs=out_specs,
        out_shape=[jax.ShapeDtypeStruct((nbatch * t, GLA_VAL_WIDTH), BF16),
                   jax.ShapeDtypeStruct((nbatch, GLA_HEADS, GLA_DK, GLA_DV), F32)],
        scratch_shapes=[pltpu.VMEM((nseq, GLA_HEADS, GLA_DK, GLA_DV), F32)],
        compiler_params=_cparams(("arbitrary", "arbitrary")),
        name="gla",
    )(z, z, z, z, z, w_alpha_p, b_alpha, norm_g, s0)


MERGE_TM = 256
ROUTER_COL_E = 32


def _dot3(a, w_parts):
    a_hi = a.astype(BF16)
    a_lo = (a - a_hi.astype(F32)).astype(BF16)
    w_hi, w_lo = w_parts
    return (jnp.dot(a_hi, w_hi, preferred_element_type=F32)
            + jnp.dot(a_lo, w_hi, preferred_element_type=F32)
            + jnp.dot(a_hi, w_lo, preferred_element_type=F32))


def _merge_kernel(oa_ref, og_ref, ga_ref, gg_ref, x_ref, wba_ref, wbg_ref, wo_ref,
                  ln_ref, wrh_ref, wrl_ref, br_ref, h_ref, hn_ref, lg_ref):
    pa = jnp.dot(oa_ref[...], wba_ref[...], preferred_element_type=F32)
    pg = jnp.dot(og_ref[...], wbg_ref[...], preferred_element_type=F32)
    merged = jax.nn.sigmoid(ga_ref[...]) * pa + jax.nn.sigmoid(gg_ref[...]) * pg
    h = x_ref[...] + jnp.dot(merged.astype(BF16), wo_ref[...], preferred_element_type=F32)
    h_ref[...] = h
    hn = _rmsnorm_rows(h, ln_ref[...])
    hn_ref[...] = hn
    lg_ref[...] = _dot3(hn, (wrh_ref[...], wrl_ref[...])) + br_ref[...]


def _merge(o_a, o_g, z, x, wba, wbg, wo, ln2, wr_hi, wr_lo, br, tm=MERGE_TM):
    n, d = x.shape
    tm = min(tm, n)
    const = lambda i: (0, 0)
    one = pl.Buffered(1)
    in_specs = [pl.BlockSpec((tm, ATTN_WIDTH), lambda i: (i, 0)),
                pl.BlockSpec((tm, GLA_VAL_WIDTH), lambda i: (i, 0)),
                pl.BlockSpec((tm, d), lambda i: (i, COL_GATE_A // D_MODEL)),
                pl.BlockSpec((tm, d), lambda i: (i, COL_GATE_G // D_MODEL)),
                pl.BlockSpec((tm, d), lambda i: (i, 0)),
                pl.BlockSpec((ATTN_WIDTH, d), const, pipeline_mode=one),
                pl.BlockSpec((GLA_VAL_WIDTH, d), const, pipeline_mode=one),
                pl.BlockSpec((d, d), const, pipeline_mode=one),
                pl.BlockSpec((1, d), const),
                pl.BlockSpec((d, LANES), const),
                pl.BlockSpec((d, LANES), const),
                pl.BlockSpec((1, LANES), const)]
    out_specs = [pl.BlockSpec((tm, d), lambda i: (i, 0)),
                 pl.BlockSpec((tm, d), lambda i: (i, 0)),
                 pl.BlockSpec((tm, LANES), lambda i: (i, 0))]
    return pl.pallas_call(
        _merge_kernel,
        grid=(n // tm,),
        in_specs=in_specs,
        out_specs=out_specs,
        out_shape=[jax.ShapeDtypeStruct((n, d), F32),
                   jax.ShapeDtypeStruct((n, d), F32),
                   jax.ShapeDtypeStruct((n, LANES), F32)],
        compiler_params=_cparams(("arbitrary",)),
        name="merge",
    )(o_a, o_g, z, z, x, wba, wbg, wo, ln2, wr_hi, wr_lo, br)


ROUTE_TM = 512
BIG_I = 1 << 20


def _route_kernel(lg_ref, e_ref, w_ref, cnt_ref, run_ref):
    i = pl.program_id(0)

    @pl.when(i == 0)
    def _():
        run_ref[...] = jnp.zeros_like(run_ref)

    lg = lg_ref[...]
    tm = lg.shape[0]
    col = lax.broadcasted_iota(I32, lg.shape, 1)
    gl = jnp.where(col < N_GROUPS, lg, NEG_INF)
    gmax = jnp.max(gl, axis=-1, keepdims=True)
    g_idx = jnp.min(jnp.where(gl == gmax, col, BIG_I), axis=-1, keepdims=True)
    p_g = 1.0 / jnp.sum(jnp.exp(gl - gmax), axis=-1, keepdims=True)
    lo = ROUTER_COL_E + g_idx * EXPERTS_PER_GROUP
    el = jnp.where((col >= lo) & (col < lo + EXPERTS_PER_GROUP), lg, NEG_INF)
    m1 = jnp.max(el, axis=-1, keepdims=True)
    i1 = jnp.min(jnp.where(el == m1, col, BIG_I), axis=-1, keepdims=True)
    el2 = jnp.where(col == i1, NEG_INF, el)
    m2 = jnp.max(el2, axis=-1, keepdims=True)
    i2 = jnp.min(jnp.where(el2 == m2, col, BIG_I), axis=-1, keepdims=True)
    e2 = jnp.exp(m2 - m1)
    w1 = p_g / (1.0 + e2)
    w2 = p_g * e2 / (1.0 + e2)
    oh = ((col == i1) | (col == i2)).astype(BF16)
    ri = lax.broadcasted_iota(I32, (tm, tm), 0)
    rj = lax.broadcasted_iota(I32, (tm, tm), 1)
    strict = (rj < ri).astype(BF16)
    cum = jnp.dot(strict, oh, preferred_element_type=F32) + run_ref[...]
    r1 = jnp.sum(jnp.where(col == i1, cum, 0.0), axis=-1, keepdims=True)
    r2 = jnp.sum(jnp.where(col == i2, cum, 0.0), axis=-1, keepdims=True)
    run_ref[...] = run_ref[...] + jnp.sum(oh.astype(F32), axis=0, keepdims=True)
    cnt_ref[...] = run_ref[...].astype(I32)
    e_out = jnp.where(col == 0, i1 - ROUTER_COL_E, 0)
    e_out = jnp.where(col == 1, i2 - ROUTER_COL_E, e_out)
    e_out = jnp.where(col == 2, r1.astype(I32), e_out)
    e_out = jnp.where(col == 3, r2.astype(I32), e_out)
    e_ref[...] = e_out
    w_ref[...] = jnp.where(col == 0, w1, jnp.where(col == 1, w2, 0.0))


def _route(logits, tm=ROUTE_TM):
    n = logits.shape[0]
    tm = min(tm, n)
    return pl.pallas_call(
        _route_kernel,
        grid=(n // tm,),
        in_specs=[pl.BlockSpec((tm, LANES), lambda i: (i, 0))],
        out_specs=[pl.BlockSpec((tm, LANES), lambda i: (i, 0)),
                   pl.BlockSpec((tm, LANES), lambda i: (i, 0)),
                   pl.BlockSpec((1, LANES), lambda i: (0, 0))],
        out_shape=[jax.ShapeDtypeStruct((n, LANES), I32),
                   jax.ShapeDtypeStruct((n, LANES), F32),
                   jax.ShapeDtypeStruct((1, LANES), I32)],
        scratch_shapes=[pltpu.VMEM((1, LANES), F32)],
        compiler_params=_cparams(("arbitrary",)),
        name="route",
    )(logits)


MOE_TM = 256
GATHER_UNROLL = 8


def _gather_rows(src_hbm, dst_vmem, idx_ref, base, nrows, sem):
    def issue(r, c):
        t = idx_ref[base + r]
        pltpu.make_async_copy(src_hbm.at[pl.ds(t, 1), :], dst_vmem.at[pl.ds(r, 1), :], sem).start()
        return c

    lax.fori_loop(0, nrows, issue, 0, unroll=GATHER_UNROLL)

    def drain(r, c):
        pltpu.make_async_copy(src_hbm.at[pl.ds(0, 1), :], dst_vmem.at[pl.ds(r, 1), :], sem).wait()
        return c

    lax.fori_loop(0, nrows, drain, 0, unroll=GATHER_UNROLL)


def _moe_kernel(te_ref, na_ref, tok_ref, hn_hbm, wrow_ref, wg_ref, wu_ref, wd_ref,
                y_ref, xbuf, sem):
    i = pl.program_id(0)
    tm = xbuf.shape[0]

    @pl.when(i < na_ref[0])
    def _():
        _gather_rows(hn_hbm, xbuf, tok_ref, i * tm, tm, sem.at[0])
        x = xbuf[...].astype(BF16)
        gate = jnp.dot(x, wg_ref[0].astype(BF16), preferred_element_type=F32)
        up = jnp.dot(x, wu_ref[0].astype(BF16), preferred_element_type=F32)
        hid = gate * jax.nn.sigmoid(gate) * up * wrow_ref[...]
        y_ref[...] = jnp.dot(hid.astype(BF16), wd_ref[0].astype(BF16), preferred_element_type=F32)

    @pl.when(i >= na_ref[0])
    def _():
        y_ref[...] = jnp.zeros_like(y_ref)


def _moe(hn, tile_expert, n_active, sorted_tok, sorted_w, w_gate, w_up, w_down, tm=MOE_TM):
    n, d = hn.shape
    max_tiles = tile_expert.shape[0]
    ff = w_gate.shape[-1]
    wmap = lambda i, te, na, tok: (te[i], 0, 0)
    grid_spec = pltpu.PrefetchScalarGridSpec(
        num_scalar_prefetch=3,
        grid=(max_tiles,),
        in_specs=[pl.BlockSpec(memory_space=pl.ANY),
                  pl.BlockSpec((tm, 1), lambda i, te, na, tok: (i, 0)),
                  pl.BlockSpec((1, d, ff), wmap),
                  pl.BlockSpec((1, d, ff), wmap),
                  pl.BlockSpec((1, ff, d), wmap)],
        out_specs=pl.BlockSpec((tm, d), lambda i, te, na, tok: (i, 0)),
        scratch_shapes=[pltpu.VMEM((tm, d), F32), pltpu.SemaphoreType.DMA((1,))],
    )
    return pl.pallas_call(
        _moe_kernel,
        grid_spec=grid_spec,
        out_shape=jax.ShapeDtypeStruct((max_tiles * tm, d), F32),
        compiler_params=_cparams(("arbitrary",)),
        name="moe",
    )(tile_expert, n_active, sorted_tok, hn, sorted_w, w_gate, w_up, w_down)


COMBINE_TM = 256


def _combine_kernel(d0_ref, d1_ref, y_hbm, h_ref, g_ref, o_ref, b0, b1, sem, *, final_norm):
    i = pl.program_id(0)
    tm = b0.shape[0]
    _gather_rows(y_hbm, b0, d0_ref, i * tm, tm, sem.at[0])
    _gather_rows(y_hbm, b1, d1_ref, i * tm, tm, sem.at[1])
    x = h_ref[...] + (b0[...] + b1[...])
    if final_norm:
        x = _rmsnorm_rows(x, g_ref[...])
    o_ref[...] = x


def _combine(y_sorted, dest0, dest1, h, g, final_norm, tm=COMBINE_TM):
    n, d = h.shape
    tm = min(tm, n)
    grid_spec = pltpu.PrefetchScalarGridSpec(
        num_scalar_prefetch=2,
        grid=(n // tm,),
        in_specs=[pl.BlockSpec(memory_space=pl.ANY),
                  pl.BlockSpec((tm, d), lambda i, a, b: (i, 0)),
                  pl.BlockSpec((1, d), lambda i, a, b: (0, 0))],
        out_specs=pl.BlockSpec((tm, d), lambda i, a, b: (i, 0)),
        scratch_shapes=[pltpu.VMEM((tm, d), F32), pltpu.VMEM((tm, d), F32),
                        pltpu.SemaphoreType.DMA((2,))],
    )
    return pl.pallas_call(
        functools.partial(_combine_kernel, final_norm=final_norm),
        grid_spec=grid_spec,
        out_shape=jax.ShapeDtypeStruct((n, d), F32),
        compiler_params=_cparams(("arbitrary",)),
        name="combine",
    )(dest0, dest1, y_sorted, h, g)


def _sort_plan(e_out, w_out, counts, tm):
    n = e_out.shape[0]
    max_tiles = (2 * n) // tm + N_EXPERTS
    cnt = counts[0, ROUTER_COL_E:ROUTER_COL_E + N_EXPERTS]
    ntile = (cnt + tm - 1) // tm
    tile_end = jnp.cumsum(ntile)
    row_off = (tile_end - ntile) * tm
    n_active = tile_end[-1:]
    tid = jnp.arange(max_tiles, dtype=I32)
    tile_expert = jnp.searchsorted(tile_end, jnp.minimum(tid, n_active[0] - 1), side="right").astype(I32)
    e1, e2, r1, r2 = e_out[:, 0], e_out[:, 1], e_out[:, 2], e_out[:, 3]
    dest0 = row_off[e1] + r1
    dest1 = row_off[e2] + r2
    tok = jnp.arange(n, dtype=I32)
    rows = max_tiles * tm
    sorted_tok = jnp.zeros((rows,), I32).at[dest0].set(tok).at[dest1].set(tok)
    sorted_w = jnp.zeros((rows,), F32).at[dest0].set(w_out[:, 0]).at[dest1].set(w_out[:, 1])
    return tile_expert, n_active.astype(I32), sorted_tok, sorted_w.reshape(rows, 1), dest0, dest1


def _router_weights(w_rg, b_rg, w_re, b_re):
    d = w_rg.shape[0]
    w = jnp.zeros((d, LANES), F32)
    w = w.at[:, :N_GROUPS].set(w_rg).at[:, ROUTER_COL_E:ROUTER_COL_E + N_EXPERTS].set(w_re)
    b = jnp.zeros((1, LANES), F32)
    b = b.at[0, :N_GROUPS].set(b_rg).at[0, ROUTER_COL_E:ROUTER_COL_E + N_EXPERTS].set(b_re)
    w_hi = w.astype(BF16)
    w_lo = (w - w_hi.astype(F32)).astype(BF16)
    return w_hi, w_lo, b


GLA_CHUNK_PROMPT = 64
GLA_SEQ_PER_STEP = 8


def kernel(x_prompt, x_sample, cache_swa_k, cache_swa_v, state_gla, ln1_g, w_in, w_alpha, b_alpha,
           gla_norm_g, attn_sinks, w_br_attn, w_br_gla, w_out, ln2_g, w_router_group,
           b_router_group, w_router_expert, b_router_expert, w_exp_gate, w_exp_up, w_exp_down,
           final_norm_g):
    bp, tp, d = x_prompt.shape
    bs, ls, _ = x_sample.shape
    n_p = bp * tp
    n_s = bs * ls
    wb = cache_swa_k.shape[2]
    x = jnp.concatenate([x_prompt.reshape(n_p, d), x_sample.reshape(n_s, d)], axis=0)

    w_in_p = _permute_w_in(w_in)
    wba = w_br_attn.astype(BF16)
    wbg = w_br_gla.astype(BF16)
    wo = w_out.astype(BF16)
    w_alpha_p = jnp.zeros((DEPTH, LANES, GLA_KEY_WIDTH), F32).at[:, :GLA_LOWRANK, :].set(w_alpha)
    s0_prompt = jnp.zeros((bp, GLA_HEADS, GLA_DK, GLA_DV), F32)

    pk, pv, ps, sk, sv, ss = [], [], [], [], [], []
    for l in range(DEPTH):
        z = _inproj(x, ln1_g[l].reshape(1, d), w_in_p[l])
        sinks = attn_sinks[l].astype(F32)
        oa_p = _swa_prompt(z, sinks, n_p)
        oa_s, k_new_s, v_new_s = _swa_sample(
            z, sinks, cache_swa_k[l].reshape(bs, wb, KV_WIDTH), cache_swa_v[l].reshape(bs, wb, KV_WIDTH),
            n_p, bs, ls)
        ba = b_alpha[l].reshape(1, GLA_KEY_WIDTH)
        ng = gla_norm_g[l].reshape(1, GLA_DV)
        og_p, s_p = _gla(z, w_alpha_p[l], ba, ng, s0_prompt, 0, bp, tp, GLA_CHUNK_PROMPT, 1)
        og_s, s_s = _gla(z, w_alpha_p[l], ba, ng, state_gla[l], n_p, bs, ls, ls, GLA_SEQ_PER_STEP)
        o_a = jnp.concatenate([oa_p, oa_s], axis=0)
        o_g = jnp.concatenate([og_p, og_s], axis=0)
        wr_hi, wr_lo, br = _router_weights(w_router_group[l], b_router_group[l],
                                           w_router_expert[l], b_router_expert[l])
        h, hn, logits = _merge(o_a, o_g, z, x, wba[l], wbg[l], wo[l], ln2_g[l].reshape(1, d),
                               wr_hi, wr_lo, br)
        e_out, w_out_r, counts = _route(logits)
        tile_expert, n_active, sorted_tok, sorted_w, dest0, dest1 = _sort_plan(
            e_out, w_out_r, counts, MOE_TM)
        y_sorted = _moe(hn, tile_expert, n_active, sorted_tok, sorted_w,
                        w_exp_gate[l], w_exp_up[l], w_exp_down[l])
        last = l == DEPTH - 1
        x = _combine(y_sorted, dest0, dest1, h, final_norm_g.reshape(1, d), last)

        wk = min(WINDOW, tp)
        k_p = z[:n_p, COL_K_A:COL_K_A + KV_WIDTH].reshape(bp, tp, N_KV_HEADS, HEAD_DIM)[:, tp - wk:]
        v_p = z[:n_p, COL_V_A:COL_V_A + KV_WIDTH].reshape(bp, tp, N_KV_HEADS, HEAD_DIM)[:, tp - wk:]
        pk.append(k_p)
        pv.append(v_p)
        ps.append(s_p)
        sk.append(k_new_s.reshape(bs, wb, N_KV_HEADS, HEAD_DIM))
        sv.append(v_new_s.reshape(bs, wb, N_KV_HEADS, HEAD_DIM))
        ss.append(s_s)

    y_prompt = x[:n_p].reshape(bp, tp, d)
    y_sample = x[n_p:].reshape(bs, ls, d)
    return (y_prompt, y_sample, jnp.stack(pk), jnp.stack(pv), jnp.stack(ps),
            jnp.stack(sk), jnp.stack(sv), jnp.stack(ss))
```

```python
import functools

import jax
import jax.numpy as jnp
from jax import lax
from jax.experimental import pallas as pl
from jax.experimental.pallas import tpu as pltpu

F32 = jnp.float32
BF16 = jnp.bfloat16
I32 = jnp.int32

D_MODEL = 2048
DEPTH = 2
N_Q_HEADS = 16
N_KV_HEADS = 2
KV_GROUP = N_Q_HEADS // N_KV_HEADS
HEAD_DIM = 64
ATTN_WIDTH = N_Q_HEADS * HEAD_DIM
KV_WIDTH = N_KV_HEADS * HEAD_DIM
WINDOW = 128
GLA_HEADS = 4
GLA_DV = 256
GLA_DK = 128
GLA_KEY_WIDTH = GLA_HEADS * GLA_DK
GLA_VAL_WIDTH = GLA_HEADS * GLA_DV
GLA_LOWRANK = 16
GLA_TAU = 16.0
N_GROUPS = 4
EXPERTS_PER_GROUP = 8
N_EXPERTS = N_GROUPS * EXPERTS_PER_GROUP
EXPERT_FF = 512
RMS_EPS = 1e-5

LANES = 128
SUBLANES = 8
VMEM_LIMIT = 56 * 1024 * 1024

COL_GATE_A = 0
COL_GATE_G = 2048
COL_Q_A = 4096
COL_V_G = 5120
COL_R_G = 6144
COL_Q_G = 7168
COL_K_G = 7680
COL_K_A = 8192
COL_V_A = 8320
COL_A_LR = 8448
N_PROJ = 8704

_O_QA, _O_KA, _O_VA, _O_QG, _O_KG, _O_VG, _O_ALR, _O_RG, _O_GA, _O_GG, _O_END = (
    0, 1024, 1152, 1280, 1792, 2304, 3328, 3344, 4368, 6416, 8464)

NEG_INF = float("-inf")


def _cparams(sem):
    return pltpu.CompilerParams(dimension_semantics=sem, vmem_limit_bytes=VMEM_LIMIT)


_NT = (((1,), (1,)), ((), ()))


def _dot(a, b, hp, dims=None):
    if hp:
        a, b, prec = a.astype(F32), b.astype(F32), lax.Precision.HIGHEST
    else:
        a, b, prec = a.astype(BF16), b.astype(BF16), None
    if dims is None:
        return jnp.dot(a, b, precision=prec, preferred_element_type=F32)
    return lax.dot_general(a, b, dims, precision=prec, preferred_element_type=F32)


WPREP_TN = 256
WPREP_ROWS = 256
_W_SEGMENTS = ((_O_GA, 8), (_O_GG, 8), (_O_QA, 4), (_O_VG, 4), (_O_RG, 4), (_O_QG, 2), (_O_KG, 2),
               (_O_KA, 1), (_O_ALR, 1))


def _wprep_tables():
    a_idx, b_idx, shift = [], [], []
    for src0, ntiles in _W_SEGMENTS:
        for t in range(ntiles):
            src = src0 + t * WPREP_TN
            sh = src % LANES
            assert (src - sh) % WPREP_TN == 0
            a_idx.append((src - sh) // WPREP_TN)
            b_idx.append((src - sh + WPREP_TN) // LANES)
            shift.append(sh)
    assert len(a_idx) * WPREP_TN == N_PROJ and set(shift) <= {0, _O_RG % LANES}
    return (jnp.asarray(a_idx, I32), jnp.asarray(b_idx, I32), jnp.asarray(shift, I32))


def _wprep_kernel(ai_ref, bi_ref, sh_ref, a_ref, b_ref, o_ref):
    j = pl.program_id(0)
    depth, d, tn = a_ref.shape
    sh = _O_RG % LANES
    nchunk = d // WPREP_ROWS

    @pl.when(sh_ref[j] == 0)
    def _():
        def body(c, carry):
            l = c // nchunk
            rows = pl.ds(pl.multiple_of((c % nchunk) * WPREP_ROWS, WPREP_ROWS), WPREP_ROWS)
            o_ref[l, rows, :] = a_ref[l, rows, :].astype(BF16)
            return carry

        lax.fori_loop(0, depth * nchunk, body, 0)

    @pl.when(sh_ref[j] != 0)
    def _():
        def body(c, carry):
            l = c // nchunk
            rows = pl.ds(pl.multiple_of((c % nchunk) * WPREP_ROWS, WPREP_ROWS), WPREP_ROWS)
            cat = jnp.concatenate([a_ref[l, rows, :], b_ref[l, rows, :]], axis=1)
            o_ref[l, rows, :] = pltpu.roll(cat, tn + LANES - sh, 1)[:, :tn].astype(BF16)
            return carry

        lax.fori_loop(0, depth * nchunk, body, 0)


def _wprep(w_in):
    depth, d, _ = w_in.shape
    a_idx, b_idx, shift = _wprep_tables()
    grid_spec = pltpu.PrefetchScalarGridSpec(
        num_scalar_prefetch=3,
        grid=(N_PROJ // WPREP_TN,),
        in_specs=[pl.BlockSpec((depth, d, WPREP_TN), lambda j, ai, bi, sh: (0, 0, ai[j])),
                  pl.BlockSpec((depth, d, LANES), lambda j, ai, bi, sh: (0, 0, bi[j]))],
        out_specs=pl.BlockSpec((depth, d, WPREP_TN), lambda j, ai, bi, sh: (0, 0, j)),
    )
    return pl.pallas_call(
        _wprep_kernel,
        grid_spec=grid_spec,
        out_shape=jax.ShapeDtypeStruct((depth, d, N_PROJ), BF16),
        compiler_params=_cparams(("arbitrary",)),
        name="wprep",
    )(a_idx, b_idx, shift, w_in, w_in)


INPROJ_TM = 1536
INPROJ_TN = 512
NORM_ROWS = 256


def _rmsnorm_rows(x, g):
    ms = jnp.mean(x * x, axis=-1, keepdims=True)
    return x * lax.rsqrt(ms + RMS_EPS) * g


def _inproj_kernel(x_ref, g_ref, w_ref, z_ref, xn_ref):
    @pl.when(pl.program_id(1) == 0)
    def _():
        g = g_ref[...]

        def body(r, c):
            rows = pl.ds(pl.multiple_of(r * NORM_ROWS, NORM_ROWS), NORM_ROWS)
            xn_ref[rows, :] = _rmsnorm_rows(x_ref[rows, :], g).astype(BF16)
            return c

        lax.fori_loop(0, x_ref.shape[0] // NORM_ROWS, body, 0)

    z_ref[...] = jnp.dot(xn_ref[...], w_ref[...], preferred_element_type=F32)


def _inproj(x, g, w, layer, tm=INPROJ_TM, tn=INPROJ_TN):
    n, d = x.shape
    npj = w.shape[2]
    tm = min(tm, n)
    assert n % tm == 0 and tm % NORM_ROWS == 0 and npj % tn == 0
    return pl.pallas_call(
        _inproj_kernel,
        grid=(n // tm, npj // tn),
        in_specs=[pl.BlockSpec((tm, d), lambda i, j: (i, 0)),
                  pl.BlockSpec((1, d), lambda i, j: (0, 0)),
                  pl.BlockSpec((None, d, tn), lambda i, j: (layer, 0, j))],
        out_specs=pl.BlockSpec((tm, tn), lambda i, j: (i, j)),
        out_shape=jax.ShapeDtypeStruct((n, npj), F32),
        scratch_shapes=[pltpu.VMEM((tm, d), BF16)],
        compiler_params=_cparams(("arbitrary", "arbitrary")),
        name="inproj",
    )(x, g, w)


def _sink_softmax_pv(s, mask, sink, v, hp=False):
    s = jnp.where(mask, s, NEG_INF)
    m = jnp.maximum(jnp.max(s, axis=-1, keepdims=True), sink)
    p = jnp.exp(s - m)
    den = jnp.sum(p, axis=-1, keepdims=True) + jnp.exp(sink - m)
    return _dot(p, v, hp) / den


def _swa_prompt_kernel(sink_ref, q_ref, kc_ref, kp_ref, vc_ref, vp_ref, o_ref, *, hp):
    n = pl.program_id(0)
    w = WINDOW
    op_dtype = F32 if hp else BF16
    q = (q_ref[...] * (HEAD_DIM ** -0.5)).astype(op_dtype)
    kk = jnp.concatenate([kp_ref[...], kc_ref[...]], axis=0).astype(op_dtype)
    vv = jnp.concatenate([vp_ref[...], vc_ref[...]], axis=0).astype(op_dtype)
    iq = lax.broadcasted_iota(I32, (w, 2 * w), 0)
    ik = lax.broadcasted_iota(I32, (w, 2 * w), 1)
    diff = iq + w - ik
    mask = (diff >= 0) & (diff < w) & ((ik >= w) | (n > 0))
    outs = []
    for h in range(N_KV_HEADS):
        kh = kk[:, h * HEAD_DIM:(h + 1) * HEAD_DIM]
        vh = vv[:, h * HEAD_DIM:(h + 1) * HEAD_DIM]
        for g in range(KV_GROUP):
            hq = h * KV_GROUP + g
            qh = q[:, hq * HEAD_DIM:(hq + 1) * HEAD_DIM]
            s = _dot(qh, kh, hp, _NT)
            outs.append(_sink_softmax_pv(s, mask, sink_ref[hq], vh, hp))
    o_ref[...] = jnp.concatenate(outs, axis=1).astype(o_ref.dtype)


def _swa_prompt(z, sinks, t, hp=False):
    w = WINDOW
    nb = t // w
    cq, ck, cv = COL_Q_A // ATTN_WIDTH, COL_K_A // KV_WIDTH, COL_V_A // KV_WIDTH
    grid_spec = pltpu.PrefetchScalarGridSpec(
        num_scalar_prefetch=1,
        grid=(nb,),
        in_specs=[pl.BlockSpec((w, ATTN_WIDTH), lambda n, s: (n, cq)),
                  pl.BlockSpec((w, KV_WIDTH), lambda n, s: (n, ck)),
                  pl.BlockSpec((w, KV_WIDTH), lambda n, s: (jnp.maximum(n - 1, 0), ck)),
                  pl.BlockSpec((w, KV_WIDTH), lambda n, s: (n, cv)),
                  pl.BlockSpec((w, KV_WIDTH), lambda n, s: (jnp.maximum(n - 1, 0), cv))],
        out_specs=pl.BlockSpec((w, ATTN_WIDTH), lambda n, s: (n, 0)),
    )
    return pl.pallas_call(
        functools.partial(_swa_prompt_kernel, hp=hp),
        grid_spec=grid_spec,
        out_shape=jax.ShapeDtypeStruct((t, ATTN_WIDTH), F32 if hp else BF16),
        compiler_params=_cparams(("arbitrary",)),
        name="swa_prompt_hp" if hp else "swa_prompt",
    )(sinks, z, z, z, z, z)


SWA_S_BB = 8


def _swa_sample_kernel(sink_ref, q_ref, kn_ref, vn_ref, kc_ref, vc_ref,
                       o_ref, ko_ref, vo_ref, *, l):
    w = WINDOW
    bb = kc_ref.shape[0]
    kpad = 2 * w - w - l
    rows = KV_GROUP * l
    ir = lax.broadcasted_iota(I32, (rows, 2 * w), 0)
    ik = lax.broadcasted_iota(I32, (rows, 2 * w), 1)
    lq = ir & (l - 1)
    diff = w + lq - ik
    mask = (diff >= 0) & (diff < w)
    gi = lax.broadcasted_iota(I32, (rows, 1), 0) >> (l.bit_length() - 1)
    zpad = jnp.zeros((kpad, KV_WIDTH), F32)
    for b in range(bb):
        q = (q_ref[b * l:(b + 1) * l, :] * (HEAD_DIM ** -0.5)).astype(BF16)
        kn = kn_ref[b * l:(b + 1) * l, :]
        vn = vn_ref[b * l:(b + 1) * l, :]
        kc = kc_ref[b]
        vc = vc_ref[b]
        ko_ref[b] = jnp.concatenate([kc[l:, :], kn], axis=0)
        vo_ref[b] = jnp.concatenate([vc[l:, :], vn], axis=0)
        kk = jnp.concatenate([kc, kn, zpad], axis=0).astype(BF16)
        vv = jnp.concatenate([vc, vn, zpad], axis=0).astype(BF16)
        pieces = []
        for h in range(N_KV_HEADS):
            kh = kk[:, h * HEAD_DIM:(h + 1) * HEAD_DIM]
            vh = vv[:, h * HEAD_DIM:(h + 1) * HEAD_DIM]
            qh = jnp.concatenate(
                [q[:, (h * KV_GROUP + g) * HEAD_DIM:(h * KV_GROUP + g + 1) * HEAD_DIM]
                 for g in range(KV_GROUP)], axis=0)
            sink = jnp.zeros((rows, 1), F32)
            for g in range(KV_GROUP):
                sink = jnp.where(gi == g, sink_ref[h * KV_GROUP + g], sink)
            s = lax.dot_general(qh, kh, (((1,), (1,)), ((), ())), preferred_element_type=F32)
            o = _sink_softmax_pv(s, mask, sink, vh)
            pieces.extend(o[g * l:(g + 1) * l, :] for g in range(KV_GROUP))
        o_ref[b * l:(b + 1) * l, :] = jnp.concatenate(pieces, axis=1).astype(o_ref.dtype)


def _swa_sample(z, sinks, k_cache, v_cache, row0, nseq, l):
    assert l & (l - 1) == 0
    w = WINDOW
    bb = SWA_S_BB
    rb = bb * l
    r0 = row0 // rb
    cq = COL_Q_A // ATTN_WIDTH
    ck, cv = COL_K_A // KV_WIDTH, COL_V_A // KV_WIDTH
    grid_spec = pltpu.PrefetchScalarGridSpec(
        num_scalar_prefetch=1,
        grid=(nseq // bb,),
        in_specs=[pl.BlockSpec((rb, ATTN_WIDTH), lambda i, s: (r0 + i, cq)),
                  pl.BlockSpec((rb, KV_WIDTH), lambda i, s: (r0 + i, ck)),
                  pl.BlockSpec((rb, KV_WIDTH), lambda i, s: (r0 + i, cv)),
                  pl.BlockSpec((bb, w, KV_WIDTH), lambda i, s: (i, 0, 0)),
                  pl.BlockSpec((bb, w, KV_WIDTH), lambda i, s: (i, 0, 0))],
        out_specs=[pl.BlockSpec((rb, ATTN_WIDTH), lambda i, s: (i, 0)),
                   pl.BlockSpec((bb, w, KV_WIDTH), lambda i, s: (i, 0, 0)),
                   pl.BlockSpec((bb, w, KV_WIDTH), lambda i, s: (i, 0, 0))],
    )
    return pl.pallas_call(
        functools.partial(_swa_sample_kernel, l=l),
        grid_spec=grid_spec,
        out_shape=[jax.ShapeDtypeStruct((nseq * l, ATTN_WIDTH), BF16),
                   jax.ShapeDtypeStruct((nseq, w, KV_WIDTH), F32),
                   jax.ShapeDtypeStruct((nseq, w, KV_WIDTH), F32)],
        compiler_params=_cparams(("arbitrary",)),
        name="swa_sample",
    )(sinks, z, z, z, k_cache, v_cache)


GLA_SUB = SUBLANES


def _split3(x):
    hi = x.astype(BF16)
    r1 = x - hi.astype(F32)
    mid = r1.astype(BF16)
    lo = (r1 - mid.astype(F32)).astype(BF16)
    return hi, mid, lo


def _gla_kernel(*refs, c, nseq, hp, chained, snap_chunk):
    q_ref, k_ref, v_ref, a_ref, r_ref, wa_ref, ba_ref, ng_ref, s0_ref = refs[:9]
    outs_at = 10 if chained else 9
    o_ref, so_ref = refs[outs_at:outs_at + 2]
    snap_ref = refs[outs_at + 2] if snap_chunk is not None else None
    s_scr = refs[-1]
    ci = pl.program_id(1)
    nsb = c // GLA_SUB
    scale = GLA_DK ** -0.5

    @pl.when(ci == 0)
    def _():
        s_scr[...] = s0_ref[...]

    if snap_ref is not None:
        @pl.when(ci == snap_chunk)
        def _():
            snap_ref[...] = s_scr[...]

    rb = nseq * c
    row = lax.broadcasted_iota(I32, (c, 1), 0)
    rsub = row & (GLA_SUB - 1)
    ri = lax.broadcasted_iota(I32, (rb, rb), 0)
    rj = lax.broadcasted_iota(I32, (rb, rb), 1)
    tri = ((rj <= ri) & (rj >= ri - (ri & (c - 1)))).astype(BF16)
    row_l = lax.broadcasted_iota(I32, (LANES, GLA_DK), 0)

    pre = _dot(a_ref[...], wa_ref[...], hp) + ba_ref[...]
    log_a = jax.nn.log_sigmoid(pre) / GLA_TAU
    g_hi, g_mid, g_lo = _split3(log_a)
    b_full = (jnp.dot(tri, g_hi, preferred_element_type=F32)
              + jnp.dot(tri, g_mid, preferred_element_type=F32)
              + jnp.dot(tri, g_lo, preferred_element_type=F32))

    for sq in range(nseq):
        rows = slice(sq * c, (sq + 1) * c)
        b_all = b_full[rows, :]
        outs = []
        for h in range(GLA_HEADS):
            ks = slice(h * GLA_DK, (h + 1) * GLA_DK)
            vs = slice(h * GLA_DV, (h + 1) * GLA_DV)
            q = q_ref[rows, ks] * scale
            k = k_ref[rows, ks]
            v = v_ref[rows, vs]
            b = b_all[:, ks]
            s_prev = s_scr[sq, h]
            v_op = v.astype(F32 if hp else BF16)

            o = _dot(q * jnp.exp(b), s_prev, hp)

            if nsb > 1:
                blocks = [jnp.zeros((GLA_SUB, c), F32)]
                for i in range(1, nsb):
                    r0 = i * GLA_SUB
                    c_i = b[r0:r0 + 1, :]
                    qt = q[r0:r0 + GLA_SUB, :] * jnp.exp(b[r0:r0 + GLA_SUB, :] - c_i)
                    kt = k * jnp.exp(jnp.where(row < r0, c_i - b, NEG_INF))
                    blocks.append(_dot(qt, kt, hp, _NT))
                attn = jnp.concatenate(blocks, axis=0)
                o = o + _dot(attn, v_op, hp)

            for d in range(GLA_SUB):
                if d == 0:
                    wgt = jnp.sum(q * k, axis=-1, keepdims=True)
                    vr = v
                else:
                    kr = pltpu.roll(k, d, 0)
                    br = pltpu.roll(b, d, 0)
                    dec = jnp.exp(jnp.where(rsub >= d, b - br, NEG_INF))
                    wgt = jnp.sum(q * kr * dec, axis=-1, keepdims=True)
                    vr = pltpu.roll(v, d, 0)
                o = o + wgt * vr

            b_last = b[c - 1:c, :]
            kd = k * jnp.exp(b_last - b)
            e_last = jnp.exp(b_last)
            zrows = jnp.zeros((LANES - c, GLA_DK), F32)
            y = jnp.where(row_l == c, jnp.broadcast_to(e_last, (LANES, GLA_DK)),
                          jnp.concatenate([kd, zrows], axis=0))
            yt = y.T
            v_pad = jnp.concatenate([v_op, jnp.zeros((LANES - c, GLA_DV), v_op.dtype)], axis=0)
            s_scr[sq, h] = yt[:, c:c + 1] * s_prev + _dot(yt, v_pad, hp)

            r = r_ref[rows, vs]
            outs.append(_rmsnorm_rows(o, ng_ref[...]) * (r * jax.nn.sigmoid(r)))
        o_ref[rows, :] = jnp.concatenate(outs, axis=1).astype(o_ref.dtype)

    @pl.when(ci == pl.num_programs(1) - 1)
    def _():
        if so_ref.shape == s_scr.shape:
            so_ref[...] = s_scr[...]
        else:
            so_ref[0] = s_scr[...]
            so_ref[1:] = jnp.zeros((so_ref.shape[0] - 1,) + s_scr.shape, F32)


def _gla(z, w_alpha_p, b_alpha, norm_g, s0, s0_layer, layer, s_prev, row0, nbatch, t, c, nseq,
         hp=False, snap_chunk=None):
    nchunk = t // c
    assert nseq == 1 or nchunk == 1
    assert c & (c - 1) == 0 and c % GLA_SUB == 0 and c < LANES
    rb = nseq * c
    r0 = row0 // rb
    rmap = lambda j: (lambda bi, ci: (r0 + bi * nchunk + ci, j))
    state_block = (None, nseq, GLA_HEADS, GLA_DK, GLA_DV)
    state_shape = (nbatch, GLA_HEADS, GLA_DK, GLA_DV)
    in_specs = [pl.BlockSpec((rb, GLA_KEY_WIDTH), rmap(COL_Q_G // GLA_KEY_WIDTH)),
                pl.BlockSpec((rb, GLA_KEY_WIDTH), rmap(COL_K_G // GLA_KEY_WIDTH)),
                pl.BlockSpec((rb, GLA_VAL_WIDTH), rmap(COL_V_G // GLA_VAL_WIDTH)),
                pl.BlockSpec((rb, LANES), rmap(COL_A_LR // LANES)),
                pl.BlockSpec((rb, GLA_VAL_WIDTH), rmap(COL_R_G // GLA_VAL_WIDTH)),
                pl.BlockSpec((None, LANES, GLA_KEY_WIDTH), lambda bi, ci: (layer, 0, 0)),
                pl.BlockSpec((1, GLA_KEY_WIDTH), lambda bi, ci: (0, 0)),
                pl.BlockSpec((1, GLA_DV), lambda bi, ci: (0, 0)),
                pl.BlockSpec(state_block, lambda bi, ci: (s0_layer, bi, 0, 0, 0))]
    args = [z, z, z, z, z, w_alpha_p, b_alpha, norm_g, s0]
    aliases = {}
    if s_prev is None:
        assert layer == 0
        state_out = pl.BlockSpec((DEPTH,) + state_block[1:], lambda bi, ci: (0, bi, 0, 0, 0))
    else:
        in_specs.append(pl.BlockSpec(memory_space=pl.ANY))
        args.append(s_prev)
        aliases = {len(args) - 1: 1}
        state_out = pl.BlockSpec(state_block, lambda bi, ci: (layer, bi, 0, 0, 0))
    out_specs = [pl.BlockSpec((rb, GLA_VAL_WIDTH), lambda bi, ci: (bi * nchunk + ci, 0)), state_out]
    out_shape = [jax.ShapeDtypeStruct((nbatch * t, GLA_VAL_WIDTH), F32 if hp else BF16),
                 jax.ShapeDtypeStruct((DEPTH,) + state_shape, F32)]
    if snap_chunk is not None:
        assert 0 <= snap_chunk < nchunk
        out_specs.append(pl.BlockSpec(state_block, lambda bi, ci: (0, bi, 0, 0, 0)))
        out_shape.append(jax.ShapeDtypeStruct((1,) + state_shape, F32))
    return pl.pallas_call(
        functools.partial(_gla_kernel, c=c, nseq=nseq, hp=hp, chained=s_prev is not None,
                          snap_chunk=snap_chunk),
        grid=(nbatch // nseq, nchunk),
        in_specs=in_specs,
        out_specs=out_specs,
        out_shape=out_shape,
        scratch_shapes=[pltpu.VMEM((nseq,) + state_shape[1:], F32)],
        input_output_aliases=aliases,
        compiler_params=_cparams(("arbitrary", "arbitrary")),
        name="gla_hp" if hp else "gla",
    )(*args)


MERGE_TM = 256
ROUTER_COL_E = 32


def _merge_kernel(oa_ref, og_ref, ga_ref, gg_ref, x_ref, wba_ref, wbg_ref, wo_ref,
                  ln_ref, wr_ref, br_ref, h_ref, hn_ref, lg_ref):
    pa = jnp.dot(oa_ref[...], wba_ref[...], preferred_element_type=F32)
    pg = jnp.dot(og_ref[...], wbg_ref[...], preferred_element_type=F32)
    merged = jax.nn.sigmoid(ga_ref[...]) * pa + jax.nn.sigmoid(gg_ref[...]) * pg
    h = x_ref[...] + jnp.dot(merged.astype(BF16), wo_ref[...], preferred_element_type=F32)
    h_ref[...] = h
    hn = _rmsnorm_rows(h, ln_ref[...])
    hn_ref[...] = hn
    lg_ref[...] = _dot(hn, wr_ref[...], True) + br_ref[...]


def _merge(o_a, o_g, z, x, wba, wbg, wo, ln2, wr, br, tm=MERGE_TM):
    n, d = x.shape
    tm = min(tm, n)
    const = lambda i: (0, 0)
    one = pl.Buffered(1)
    in_specs = [pl.BlockSpec((tm, ATTN_WIDTH), lambda i: (i, 0)),
                pl.BlockSpec((tm, GLA_VAL_WIDTH), lambda i: (i, 0)),
                pl.BlockSpec((tm, d), lambda i: (i, COL_GATE_A // D_MODEL)),
                pl.BlockSpec((tm, d), lambda i: (i, COL_GATE_G // D_MODEL)),
                pl.BlockSpec((tm, d), lambda i: (i, 0)),
                pl.BlockSpec((ATTN_WIDTH, d), const, pipeline_mode=one),
                pl.BlockSpec((GLA_VAL_WIDTH, d), const, pipeline_mode=one),
                pl.BlockSpec((d, d), const, pipeline_mode=one),
                pl.BlockSpec((1, d), const),
                pl.BlockSpec((d, LANES), const),
                pl.BlockSpec((1, LANES), const)]
    out_specs = [pl.BlockSpec((tm, d), lambda i: (i, 0)),
                 pl.BlockSpec((tm, d), lambda i: (i, 0)),
                 pl.BlockSpec((tm, LANES), lambda i: (i, 0))]
    return pl.pallas_call(
        _merge_kernel,
        grid=(n // tm,),
        in_specs=in_specs,
        out_specs=out_specs,
        out_shape=[jax.ShapeDtypeStruct((n, d), F32),
                   jax.ShapeDtypeStruct((n, d), F32),
                   jax.ShapeDtypeStruct((n, LANES), F32)],
        compiler_params=_cparams(("arbitrary",)),
        name="merge",
    )(o_a, o_g, z, z, x, wba, wbg, wo, ln2, wr, br)


TAIL_ROWS = 256
TAIL_HISTORY = 256
HP_TN = 512


def _prep_identity(a):
    return a


def _prep_rmsnorm(x, g):
    return _rmsnorm_rows(x, g)


def _prep_gated_sum(pa, pg, ga, gg):
    return jax.nn.sigmoid(ga) * pa + jax.nn.sigmoid(gg) * pg


def _mm_hp_kernel(*refs, prep, n):
    *a_refs, b_ref, o_ref = refs
    a = prep(*[r[...] for r in a_refs])
    o = _dot(a, b_ref[...], True)
    tn = o.shape[1]
    if n % tn:
        col = pl.program_id(0) * tn + lax.broadcasted_iota(I32, o.shape, 1)
        o = jnp.where(col < n, o, 0.0)
    o_ref[...] = o


def _mm_hp(a_args, a_specs, prep, w, layer, m, tn=HP_TN):
    _, k, n = w.shape
    nt = pl.cdiv(n, tn)
    return pl.pallas_call(
        functools.partial(_mm_hp_kernel, prep=prep, n=n),
        grid=(nt,),
        in_specs=list(a_specs) + [pl.BlockSpec((None, k, tn), lambda j: (layer, 0, j))],
        out_specs=pl.BlockSpec((m, tn), lambda j: (0, j)),
        out_shape=jax.ShapeDtypeStruct((m, nt * tn), F32),
        compiler_params=_cparams(("arbitrary",)),
        name="mm_hp",
    )(*a_args, w)


def _whole(shape):
    return pl.BlockSpec(shape, lambda j: (0,) * len(shape))


def _tail_router_kernel(x_ref, u_ref, ln_ref, wr_ref, br_ref, lg_ref):
    hn = _rmsnorm_rows(x_ref[...] + u_ref[...], ln_ref[...])
    lg_ref[...] = _dot(hn, wr_ref[...], True) + br_ref[...]


def _permute_cols(z_o):
    s = lambda a, b: z_o[:, a:b]
    pad = jnp.zeros((z_o.shape[0], N_PROJ - COL_A_LR - GLA_LOWRANK), z_o.dtype)
    return jnp.concatenate(
        [s(_O_GA, _O_GG), s(_O_GG, _O_END), s(_O_QA, _O_KA), s(_O_VG, _O_ALR), s(_O_RG, _O_GA),
         s(_O_QG, _O_KG), s(_O_KG, _O_VG), s(_O_KA, _O_VA), s(_O_VA, _O_QG), s(_O_ALR, _O_RG), pad],
        axis=-1)


def _tail_logits(x_tail, keep, snap, ln1, w_in, sinks, w_alpha_p, b_alpha, norm_g, w_br_attn,
                 w_br_gla, w_out, ln2, wr, br):
    m, d = x_tail.shape
    lo = m - keep
    z_o = _mm_hp([x_tail, ln1], [_whole((m, d)), _whole((1, d))], _prep_rmsnorm, w_in, 0, m)
    z = _permute_cols(z_o)
    oa = _swa_prompt(z, sinks, m, hp=True)
    og, _ = _gla(z, w_alpha_p, b_alpha, norm_g, snap, 0, 0, None, 0, 1, m, GLA_CHUNK_PROMPT, 1, hp=True)
    pa = _mm_hp([oa[lo:]], [_whole((keep, ATTN_WIDTH))], _prep_identity, w_br_attn, 0, keep)
    pg = _mm_hp([og[lo:]], [_whole((keep, GLA_VAL_WIDTH))], _prep_identity, w_br_gla, 0, keep)
    gates = z[lo:, :COL_Q_A]
    u = _mm_hp([pa, pg, gates, gates],
               [_whole((keep, d)), _whole((keep, d)),
                pl.BlockSpec((keep, d), lambda j: (0, COL_GATE_A // D_MODEL)),
                pl.BlockSpec((keep, d), lambda j: (0, COL_GATE_G // D_MODEL))],
               _prep_gated_sum, w_out, 0, keep)
    return pl.pallas_call(
        _tail_router_kernel,
        out_shape=jax.ShapeDtypeStruct((keep, LANES), F32),
        compiler_params=pltpu.CompilerParams(vmem_limit_bytes=VMEM_LIMIT),
        name="tail_router",
    )(x_tail[lo:], u, ln2, wr, br)


ROUTE_TM = 512
BIG_I = 1 << 20


def _route_kernel(lg_ref, e_ref, w_ref, cnt_ref, run_ref):
    i = pl.program_id(0)

    @pl.when(i == 0)
    def _():
        run_ref[...] = jnp.zeros_like(run_ref)

    lg = lg_ref[...]
    tm = lg.shape[0]
    col = lax.broadcasted_iota(I32, lg.shape, 1)
    gl = jnp.where(col < N_GROUPS, lg, NEG_INF)
    gmax = jnp.max(gl, axis=-1, keepdims=True)
    g_idx = jnp.min(jnp.where(gl == gmax, col, BIG_I), axis=-1, keepdims=True)
    p_g = 1.0 / jnp.sum(jnp.exp(gl - gmax), axis=-1, keepdims=True)
    lo = ROUTER_COL_E + g_idx * EXPERTS_PER_GROUP
    el = jnp.where((col >= lo) & (col < lo + EXPERTS_PER_GROUP), lg, NEG_INF)
    m1 = jnp.max(el, axis=-1, keepdims=True)
    i1 = jnp.min(jnp.where(el == m1, col, BIG_I), axis=-1, keepdims=True)
    el2 = jnp.where(col == i1, NEG_INF, el)
    m2 = jnp.max(el2, axis=-1, keepdims=True)
    i2 = jnp.min(jnp.where(el2 == m2, col, BIG_I), axis=-1, keepdims=True)
    e2 = jnp.exp(m2 - m1)
    w1 = p_g / (1.0 + e2)
    w2 = p_g * e2 / (1.0 + e2)
    oh = ((col == i1) | (col == i2)).astype(BF16)
    ri = lax.broadcasted_iota(I32, (tm, tm), 0)
    rj = lax.broadcasted_iota(I32, (tm, tm), 1)
    strict = (rj < ri).astype(BF16)
    cum = jnp.dot(strict, oh, preferred_element_type=F32) + run_ref[...]
    r1 = jnp.sum(jnp.where(col == i1, cum, 0.0), axis=-1, keepdims=True)
    r2 = jnp.sum(jnp.where(col == i2, cum, 0.0), axis=-1, keepdims=True)
    run_ref[...] = run_ref[...] + jnp.sum(oh.astype(F32), axis=0, keepdims=True)
    cnt_ref[...] = run_ref[...].astype(I32)
    e_out = jnp.where(col == 0, i1 - ROUTER_COL_E, 0)
    e_out = jnp.where(col == 1, i2 - ROUTER_COL_E, e_out)
    e_out = jnp.where(col == 2, r1.astype(I32), e_out)
    e_out = jnp.where(col == 3, r2.astype(I32), e_out)
    e_ref[...] = e_out
    w_ref[...] = jnp.where(col == 0, w1, jnp.where(col == 1, w2, 0.0))


def _route(logits, tm=ROUTE_TM):
    n = logits.shape[0]
    tm = min(tm, n)
    return pl.pallas_call(
        _route_kernel,
        grid=(n // tm,),
        in_specs=[pl.BlockSpec((tm, LANES), lambda i: (i, 0))],
        out_specs=[pl.BlockSpec((tm, LANES), lambda i: (i, 0)),
                   pl.BlockSpec((tm, LANES), lambda i: (i, 0)),
                   pl.BlockSpec((1, LANES), lambda i: (0, 0))],
        out_shape=[jax.ShapeDtypeStruct((n, LANES), I32),
                   jax.ShapeDtypeStruct((n, LANES), F32),
                   jax.ShapeDtypeStruct((1, LANES), I32)],
        scratch_shapes=[pltpu.VMEM((1, LANES), F32)],
        compiler_params=_cparams(("arbitrary",)),
        name="route",
    )(logits)


MOE_TM = 256
MOE_SLOTS = 3
MOE_LOOKAHEAD = MOE_SLOTS - 1
GATHER_UNROLL = 8


def _row_copy(src_hbm, dst_vmem, src_row, dst_row, sem):
    return pltpu.make_async_copy(src_hbm.at[pl.ds(src_row, 1), :], dst_vmem.at[pl.ds(dst_row, 1), :], sem)


def _gather_start(src_hbm, dst_vmem, idx_ref, base, nrows, sem, static):
    if static:
        for r in range(nrows):
            _row_copy(src_hbm, dst_vmem, idx_ref[base + r], r, sem).start()
    else:
        def issue(r, c):
            _row_copy(src_hbm, dst_vmem, idx_ref[base + r], r, sem).start()
            return c

        lax.fori_loop(0, nrows, issue, 0, unroll=GATHER_UNROLL)


def _gather_wait(src_hbm, dst_vmem, nrows, sem):
    for r in range(nrows):
        _row_copy(src_hbm, dst_vmem, 0, r, sem).wait()


def _moe_kernel(te_ref, na_ref, tok_ref, hn_hbm, wg_ref, wu_ref, wd_ref, y_ref, xbuf, sem):
    i = pl.program_id(0)
    tm = xbuf.shape[1]
    na = na_ref[0]

    def start(tile, static):
        slot = lax.rem(tile, MOE_SLOTS)
        _gather_start(hn_hbm, xbuf.at[slot], tok_ref, tile * tm, tm, sem.at[slot], static)

    def wait(tile):
        slot = lax.rem(tile, MOE_SLOTS)
        _gather_wait(hn_hbm, xbuf.at[slot], tm, sem.at[slot])
        return slot

    @pl.when(i == 0)
    def _():
        for t in range(MOE_LOOKAHEAD):
            start(t, False)

    @pl.when(i < na)
    def _():
        slot = wait(i)
        x = xbuf[slot].astype(BF16)
        gate = jnp.dot(x, wg_ref[...].astype(BF16), preferred_element_type=F32)
        up = jnp.dot(x, wu_ref[...].astype(BF16), preferred_element_type=F32)
        hid = gate * jax.nn.sigmoid(gate) * up
        y_ref[...] = jnp.dot(hid.astype(BF16), wd_ref[...].astype(BF16), preferred_element_type=F32)
        start(i + MOE_LOOKAHEAD, True)

    @pl.when(i >= na)
    def _():
        y_ref[...] = jnp.zeros_like(y_ref)

    @pl.when((i >= na) & (i < na + MOE_LOOKAHEAD))
    def _():
        wait(i)


def _moe(hn, tile_expert, n_active, sorted_tok, w_gate, w_up, w_down, layer, tm=MOE_TM):
    n, d = hn.shape
    steps = tile_expert.shape[0]
    assert sorted_tok.shape[0] == steps * tm
    ff = w_gate.shape[-1]
    wmap = lambda i, te, na, tok: (layer, te[i], 0, 0)
    grid_spec = pltpu.PrefetchScalarGridSpec(
        num_scalar_prefetch=3,
        grid=(steps,),
        in_specs=[pl.BlockSpec(memory_space=pl.ANY),
                  pl.BlockSpec((None, None, d, ff), wmap),
                  pl.BlockSpec((None, None, d, ff), wmap),
                  pl.BlockSpec((None, None, ff, d), wmap)],
        out_specs=pl.BlockSpec((tm, d), lambda i, te, na, tok: (i, 0)),
        scratch_shapes=[pltpu.VMEM((MOE_SLOTS, tm, d), F32), pltpu.SemaphoreType.DMA((MOE_SLOTS,))],
    )
    return pl.pallas_call(
        _moe_kernel,
        grid_spec=grid_spec,
        out_shape=jax.ShapeDtypeStruct((steps * tm, d), F32),
        compiler_params=_cparams(("arbitrary",)),
        name="moe",
    )(tile_expert, n_active, sorted_tok, hn, w_gate, w_up, w_down)


COMBINE_TM = 256


def _combine_kernel(d0_ref, d1_ref, y_hbm, h_ref, w_ref, g_ref, o_ref, buf, sem, *, final_norm):
    i = pl.program_id(0)
    tm = buf.shape[2]

    def start(tile):
        slot = lax.rem(tile, 2)
        _gather_start(y_hbm, buf.at[slot, 0], d0_ref, tile * tm, tm, sem.at[slot, 0], False)
        _gather_start(y_hbm, buf.at[slot, 1], d1_ref, tile * tm, tm, sem.at[slot, 1], False)

    @pl.when(i == 0)
    def _():
        start(0)

    @pl.when(i + 1 < pl.num_programs(0))
    def _():
        start(i + 1)

    slot = lax.rem(i, 2)
    _gather_wait(y_hbm, buf.at[slot, 0], tm, sem.at[slot, 0])
    _gather_wait(y_hbm, buf.at[slot, 1], tm, sem.at[slot, 1])
    w = w_ref[...]
    x = h_ref[...] + (w[:, 0:1] * buf[slot, 0] + w[:, 1:2] * buf[slot, 1])
    if final_norm:
        x = _rmsnorm_rows(x, g_ref[...])
    o_ref[...] = x


def _combine(y_sorted, dest0, dest1, h, w_pair, g, final_norm, tm=COMBINE_TM):
    n, d = h.shape
    tm = min(tm, n)
    grid_spec = pltpu.PrefetchScalarGridSpec(
        num_scalar_prefetch=2,
        grid=(n // tm,),
        in_specs=[pl.BlockSpec(memory_space=pl.ANY),
                  pl.BlockSpec((tm, d), lambda i, a, b: (i, 0)),
                  pl.BlockSpec((tm, LANES), lambda i, a, b: (i, 0)),
                  pl.BlockSpec((1, d), lambda i, a, b: (0, 0))],
        out_specs=pl.BlockSpec((tm, d), lambda i, a, b: (i, 0)),
        scratch_shapes=[pltpu.VMEM((2, 2, tm, d), F32), pltpu.SemaphoreType.DMA((2, 2))],
    )
    return pl.pallas_call(
        functools.partial(_combine_kernel, final_norm=final_norm),
        grid_spec=grid_spec,
        out_shape=jax.ShapeDtypeStruct((n, d), F32),
        compiler_params=_cparams(("arbitrary",)),
        name="combine",
    )(dest0, dest1, y_sorted, h, w_pair, g)


DISPATCH_TM = 512
META_NACTIVE_LANE = LANES - 1


def _dispatch_kernel(e_ref, cnt_ref, dest_ref, meta_ref, *, tm_moe):
    lane = lax.broadcasted_iota(I32, (1, LANES), 1)
    is_e = (lane >= ROUTER_COL_E) & (lane < ROUTER_COL_E + N_EXPERTS)
    shift = tm_moe.bit_length() - 1
    ntile = jnp.where(is_e, (cnt_ref[...] + (tm_moe - 1)) >> shift, 0)
    r = lax.broadcasted_iota(I32, (LANES, LANES), 0)
    c = lax.broadcasted_iota(I32, (LANES, LANES), 1)
    incl = (r <= c).astype(BF16)
    nt_rows = jnp.broadcast_to(ntile.astype(F32), (SUBLANES, LANES)).astype(BF16)
    tile_end = jnp.dot(nt_rows, incl, preferred_element_type=F32)[0:1, :]
    row_off = (tile_end - ntile.astype(F32)) * tm_moe
    n_active = jnp.max(tile_end, axis=-1, keepdims=True)
    te_col = jnp.sum(jnp.where(r == c, jnp.broadcast_to(tile_end, (LANES, LANES)), 0.0),
                     axis=1, keepdims=True)
    tid = jnp.minimum(c.astype(F32), n_active - 1.0)
    before = (te_col <= tid) & (r >= ROUTER_COL_E) & (r < ROUTER_COL_E + N_EXPERTS)
    tile_expert = jnp.sum(before.astype(F32), axis=0, keepdims=True)
    meta = jnp.where(lane == META_NACTIVE_LANE, n_active, tile_expert)
    meta_ref[...] = meta.astype(I32)

    e = e_ref[...]
    col = lax.broadcasted_iota(I32, e.shape, 1)

    def dest(expert, rank):
        off = jnp.sum(jnp.where(col == expert + ROUTER_COL_E, row_off, 0.0), axis=-1, keepdims=True)
        return off.astype(I32) + rank

    d0 = dest(e[:, 0:1], e[:, 2:3])
    d1 = dest(e[:, 1:2], e[:, 3:4])
    dest_ref[...] = jnp.where(col == 0, d0, jnp.where(col == 1, d1, 0))


def _dispatch(e_out, counts, tm_moe, tm=DISPATCH_TM):
    n = e_out.shape[0]
    tm = min(tm, n)
    assert tm_moe & (tm_moe - 1) == 0 and (2 * n) // tm_moe + N_EXPERTS + MOE_LOOKAHEAD < META_NACTIVE_LANE
    return pl.pallas_call(
        functools.partial(_dispatch_kernel, tm_moe=tm_moe),
        grid=(n // tm,),
        in_specs=[pl.BlockSpec((tm, LANES), lambda i: (i, 0)),
                  pl.BlockSpec((1, LANES), lambda i: (0, 0))],
        out_specs=[pl.BlockSpec((tm, LANES), lambda i: (i, 0)),
                   pl.BlockSpec((1, LANES), lambda i: (0, 0))],
        out_shape=[jax.ShapeDtypeStruct((n, LANES), I32),
                   jax.ShapeDtypeStruct((1, LANES), I32)],
        compiler_params=_cparams(("arbitrary",)),
        name="dispatch",
    )(e_out, counts)


PLAN_UNROLL = 8


def _plan_kernel(d0_ref, d1_ref, tok_ref):
    def clear(r, c):
        tok_ref[r] = 0
        return c

    lax.fori_loop(0, tok_ref.shape[0], clear, 0, unroll=PLAN_UNROLL)

    def place(t, c):
        tok_ref[d0_ref[t]] = t
        tok_ref[d1_ref[t]] = t
        return c

    lax.fori_loop(0, d0_ref.shape[0], place, 0, unroll=PLAN_UNROLL)


def _plan(dest0, dest1, rows):
    smem = pl.BlockSpec(memory_space=pltpu.SMEM)
    return pl.pallas_call(
        _plan_kernel,
        in_specs=[smem, smem],
        out_specs=smem,
        out_shape=jax.ShapeDtypeStruct((rows,), I32),
        name="plan",
    )(dest0, dest1)


def _router_weights(w_rg, b_rg, w_re, b_re):
    d = w_rg.shape[0]
    w = jnp.zeros((d, LANES), F32)
    w = w.at[:, :N_GROUPS].set(w_rg).at[:, ROUTER_COL_E:ROUTER_COL_E + N_EXPERTS].set(w_re)
    b = jnp.zeros((1, LANES), F32)
    b = b.at[0, :N_GROUPS].set(b_rg).at[0, ROUTER_COL_E:ROUTER_COL_E + N_EXPERTS].set(b_re)
    return w, b


GLA_CHUNK_PROMPT = 64
GLA_SEQ_PER_STEP = 8


def kernel(x_prompt, x_sample, cache_swa_k, cache_swa_v, state_gla, ln1_g, w_in, w_alpha, b_alpha,
           gla_norm_g, attn_sinks, w_br_attn, w_br_gla, w_out, ln2_g, w_router_group,
           b_router_group, w_router_expert, b_router_expert, w_exp_gate, w_exp_up, w_exp_down,
           final_norm_g):
    bp, tp, d = x_prompt.shape
    bs, ls, _ = x_sample.shape
    n_p = bp * tp
    n_s = bs * ls
    wb = cache_swa_k.shape[2]
    n = n_p + n_s
    x = jnp.concatenate([x_prompt.reshape(n_p, d), x_sample.reshape(n_s, d)], axis=0)

    w_in_p = _wprep(w_in)
    wba = w_br_attn.astype(BF16)
    wbg = w_br_gla.astype(BF16)
    wo = w_out.astype(BF16)
    w_alpha_p = jnp.zeros((DEPTH, LANES, GLA_KEY_WIDTH), F32).at[:, :GLA_LOWRANK, :].set(w_alpha)
    s0_prompt = jnp.zeros((1, bp, GLA_HEADS, GLA_DK, GLA_DV), F32)
    moe_steps = (2 * n) // MOE_TM + N_EXPERTS + MOE_LOOKAHEAD
    assert bp == 1, "the full-precision tail assumes one prompt sequence"
    tail_keep = min(TAIL_ROWS, tp)
    tail0 = max(tp - tail_keep - TAIL_HISTORY, 0)

    pk, pv, sk, sv = [], [], [], []
    s_p = s_s = None
    for l in range(DEPTH):
        z = _inproj(x, ln1_g[l].reshape(1, d), w_in_p, l)
        sinks = attn_sinks[l].astype(F32)
        oa_p = _swa_prompt(z, sinks, n_p)
        oa_s, k_new_s, v_new_s = _swa_sample(
            z, sinks, cache_swa_k[l].reshape(bs, wb, KV_WIDTH), cache_swa_v[l].reshape(bs, wb, KV_WIDTH),
            n_p, bs, ls)
        ba = b_alpha[l].reshape(1, GLA_KEY_WIDTH)
        ng = gla_norm_g[l].reshape(1, GLA_DV)
        if l == 0:
            og_p, s_p, snap = _gla(z, w_alpha_p, ba, ng, s0_prompt, 0, l, s_p, 0, bp, tp,
                                   GLA_CHUNK_PROMPT, 1, snap_chunk=tail0 // GLA_CHUNK_PROMPT)
        else:
            og_p, s_p = _gla(z, w_alpha_p, ba, ng, s0_prompt, 0, l, s_p, 0, bp, tp, GLA_CHUNK_PROMPT, 1)
        og_s, s_s = _gla(z, w_alpha_p, ba, ng, state_gla, l, l, s_s, n_p, bs, ls, ls, GLA_SEQ_PER_STEP)
        o_a = jnp.concatenate([oa_p, oa_s], axis=0)
        o_g = jnp.concatenate([og_p, og_s], axis=0)
        wr, br = _router_weights(w_router_group[l], b_router_group[l],
                                 w_router_expert[l], b_router_expert[l])
        ln2 = ln2_g[l].reshape(1, d)
        h, hn, logits = _merge(o_a, o_g, z, x, wba[l], wbg[l], wo[l], ln2, wr, br)
        if l == 0:
            lg_tail = _tail_logits(x[tail0:n_p], tail_keep, snap, ln1_g[0].reshape(1, d), w_in, sinks,
                                   w_alpha_p, ba, ng, w_br_attn, w_br_gla, w_out, ln2, wr, br)
            logits = lax.dynamic_update_slice(logits, lg_tail, (n_p - tail_keep, 0))
        e_out, w_pair, counts = _route(logits)
        dest, meta = _dispatch(e_out, counts, MOE_TM)
        dest0, dest1 = dest[:, 0], dest[:, 1]
        tile_expert = meta[0, :moe_steps]
        n_active = meta[0, META_NACTIVE_LANE:]
        sorted_tok = _plan(dest0, dest1, moe_steps * MOE_TM)
        y_sorted = _moe(hn, tile_expert, n_active, sorted_tok, w_exp_gate, w_exp_up, w_exp_down, l)
        last = l == DEPTH - 1
        x = _combine(y_sorted, dest0, dest1, h, w_pair, final_norm_g.reshape(1, d), last)

        wk = min(WINDOW, tp)
        k_p = z[:n_p, COL_K_A:COL_K_A + KV_WIDTH].reshape(bp, tp, N_KV_HEADS, HEAD_DIM)[:, tp - wk:]
        v_p = z[:n_p, COL_V_A:COL_V_A + KV_WIDTH].reshape(bp, tp, N_KV_HEADS, HEAD_DIM)[:, tp - wk:]
        pk.append(k_p)
        pv.append(v_p)
        sk.append(k_new_s.reshape(bs, wb, N_KV_HEADS, HEAD_DIM))
        sv.append(v_new_s.reshape(bs, wb, N_KV_HEADS, HEAD_DIM))

    y_prompt = x[:n_p].reshape(bp, tp, d)
    y_sample = x[n_p:].reshape(bs, ls, d)
    return (y_prompt, y_sample, jnp.stack(pk), jnp.stack(pv), s_p,
            jnp.stack(sk), jnp.stack(sv), s_s)
```

```python
import functools

import jax
import jax.numpy as jnp
from jax import lax
from jax.experimental import pallas as pl
from jax.experimental.pallas import tpu as pltpu

F32 = jnp.float32
BF16 = jnp.bfloat16
I32 = jnp.int32

D_MODEL = 2048
DEPTH = 2
N_Q_HEADS = 16
N_KV_HEADS = 2
KV_GROUP = N_Q_HEADS // N_KV_HEADS
HEAD_DIM = 64
ATTN_WIDTH = N_Q_HEADS * HEAD_DIM
KV_WIDTH = N_KV_HEADS * HEAD_DIM
WINDOW = 128
GLA_HEADS = 4
GLA_DV = 256
GLA_DK = 128
GLA_KEY_WIDTH = GLA_HEADS * GLA_DK
GLA_VAL_WIDTH = GLA_HEADS * GLA_DV
GLA_LOWRANK = 16
GLA_TAU = 16.0
N_GROUPS = 4
EXPERTS_PER_GROUP = 8
N_EXPERTS = N_GROUPS * EXPERTS_PER_GROUP
EXPERT_FF = 512
RMS_EPS = 1e-5

LANES = 128
SUBLANES = 8
VMEM_LIMIT = 56 * 1024 * 1024

COL_GATE_A = 0
COL_GATE_G = 2048
COL_Q_A = 4096
COL_V_G = 5120
COL_R_G = 6144
COL_Q_G = 7168
COL_K_G = 7680
COL_K_A = 8192
COL_V_A = 8320
COL_A_LR = 8704
PROJ_TILE = 512
N_PROJ = 9216

_O_QA, _O_KA, _O_VA, _O_QG, _O_KG, _O_VG, _O_ALR, _O_RG, _O_GA, _O_GG, _O_END = (
    0, 1024, 1152, 1280, 1792, 2304, 3328, 3344, 4368, 6416, 8464)
_PROJ_TILE_SRC = (tuple(_O_GA + PROJ_TILE * t for t in range(4)) + tuple(_O_GG + PROJ_TILE * t for t in range(4))
                  + (_O_QA, _O_QA + PROJ_TILE, _O_VG, _O_VG + PROJ_TILE, _O_RG, _O_RG + PROJ_TILE,
                     _O_QG, _O_KG, _O_KA, _O_ALR))
assert len(_PROJ_TILE_SRC) * PROJ_TILE == N_PROJ and max(_PROJ_TILE_SRC) + PROJ_TILE <= _O_END

NEG_INF = float("-inf")


def _cparams(sem):
    return pltpu.CompilerParams(dimension_semantics=sem, vmem_limit_bytes=VMEM_LIMIT)


_NT = (((1,), (1,)), ((), ()))


def _dot(a, b, hp, dims=None):
    if hp:
        a, b, prec = a.astype(F32), b.astype(F32), lax.Precision.HIGHEST
    else:
        a, b, prec = a.astype(BF16), b.astype(BF16), None
    if dims is None:
        return jnp.dot(a, b, precision=prec, preferred_element_type=F32)
    return lax.dot_general(a, b, dims, precision=prec, preferred_element_type=F32)


def _hi_lo(x):
    hi = x.astype(BF16)
    return hi, (x - hi.astype(F32)).astype(BF16)


def _dot_split(a, b):
    a_hi, a_lo = _hi_lo(a)
    b_hi, b_lo = _hi_lo(b)
    return (jnp.dot(a_hi, b_hi, preferred_element_type=F32)
            + jnp.dot(a_lo, b_hi, preferred_element_type=F32)
            + jnp.dot(a_hi, b_lo, preferred_element_type=F32))


TOK_ROWS = D_MODEL // LANES


def _store_token_major(ref, x):
    m = x.shape[0]
    for c in range(TOK_ROWS):
        ref[pl.ds(c, m, stride=TOK_ROWS), :] = x[:, c * LANES:(c + 1) * LANES]


def _load_token_major(ref, m):
    return jnp.concatenate([ref[pl.ds(c, m, stride=TOK_ROWS), :] for c in range(TOK_ROWS)], axis=1)


INPROJ_TM = 1536
NORM_ROWS = 256


def _rmsnorm_rows(x, g):
    ms = jnp.mean(x * x, axis=-1, keepdims=True)
    return x * lax.rsqrt(ms + RMS_EPS) * g


def _inproj_kernel(src_ref, x_ref, g_ref, w_ref, z_ref, xn_ref):
    del src_ref
    @pl.when(pl.program_id(1) == 0)
    def _():
        g = g_ref[...]

        def body(r, c):
            rows = pl.ds(pl.multiple_of(r * NORM_ROWS, NORM_ROWS), NORM_ROWS)
            xn_ref[rows, :] = _rmsnorm_rows(x_ref[rows, :], g).astype(BF16)
            return c

        lax.fori_loop(0, x_ref.shape[0] // NORM_ROWS, body, 0)

    z_ref[...] = lax.dot_general(xn_ref[...], w_ref[0].astype(BF16), _NT, preferred_element_type=F32)


def _inproj(x, g, w_t, layer, tm=INPROJ_TM):
    n, d = x.shape
    tm = min(tm, n)
    assert n % tm == 0 and tm % NORM_ROWS == 0
    assert all(s % SUBLANES == 0 for s in _PROJ_TILE_SRC)
    grid_spec = pltpu.PrefetchScalarGridSpec(
        num_scalar_prefetch=1,
        grid=(n // tm, N_PROJ // PROJ_TILE),
        in_specs=[pl.BlockSpec((tm, d), lambda i, j, src: (i, 0)),
                  pl.BlockSpec((1, d), lambda i, j, src: (0, 0)),
                  pl.BlockSpec((pl.Element(1), pl.Element(PROJ_TILE), pl.Element(d)),
                               lambda i, j, src: (layer, src[j] * SUBLANES, 0))],
        out_specs=pl.BlockSpec((tm, PROJ_TILE), lambda i, j, src: (i, j)),
        scratch_shapes=[pltpu.VMEM((tm, d), BF16)],
    )
    return pl.pallas_call(
        _inproj_kernel,
        grid_spec=grid_spec,
        out_shape=jax.ShapeDtypeStruct((n, N_PROJ), F32),
        compiler_params=_cparams(("arbitrary", "arbitrary")),
        name="inproj",
    )(jnp.asarray([s // SUBLANES for s in _PROJ_TILE_SRC], I32), x, g, w_t)


def _sink_softmax_pv(s, mask, sink, v, hp=False):
    s = jnp.where(mask, s, NEG_INF)
    m = jnp.maximum(jnp.max(s, axis=-1, keepdims=True), sink)
    p = jnp.exp(s - m)
    den = jnp.sum(p, axis=-1, keepdims=True) + jnp.exp(sink - m)
    return _dot(p, v, hp) / den


def _swa_prompt_kernel(sink_ref, q_ref, kc_ref, kp_ref, vc_ref, vp_ref, o_ref, *, hp):
    n = pl.program_id(0)
    w = WINDOW
    op_dtype = F32 if hp else BF16
    q = (q_ref[...] * (HEAD_DIM ** -0.5)).astype(op_dtype)
    kk = jnp.concatenate([kp_ref[...], kc_ref[...]], axis=0).astype(op_dtype)
    vv = jnp.concatenate([vp_ref[...], vc_ref[...]], axis=0).astype(op_dtype)
    iq = lax.broadcasted_iota(I32, (w, 2 * w), 0)
    ik = lax.broadcasted_iota(I32, (w, 2 * w), 1)
    diff = iq + w - ik
    mask = (diff >= 0) & (diff < w) & ((ik >= w) | (n > 0))
    outs = []
    for h in range(N_KV_HEADS):
        kh = kk[:, h * HEAD_DIM:(h + 1) * HEAD_DIM]
        vh = vv[:, h * HEAD_DIM:(h + 1) * HEAD_DIM]
        for g in range(KV_GROUP):
            hq = h * KV_GROUP + g
            qh = q[:, hq * HEAD_DIM:(hq + 1) * HEAD_DIM]
            s = _dot(qh, kh, hp, _NT)
            outs.append(_sink_softmax_pv(s, mask, sink_ref[hq], vh, hp))
    o_ref[...] = jnp.concatenate(outs, axis=1).astype(o_ref.dtype)


def _swa_prompt(z, sinks, t, hp=False):
    w = WINDOW
    nb = t // w
    cq, ck, cv = COL_Q_A // ATTN_WIDTH, COL_K_A // KV_WIDTH, COL_V_A // KV_WIDTH
    grid_spec = pltpu.PrefetchScalarGridSpec(
        num_scalar_prefetch=1,
        grid=(nb,),
        in_specs=[pl.BlockSpec((w, ATTN_WIDTH), lambda n, s: (n, cq)),
                  pl.BlockSpec((w, KV_WIDTH), lambda n, s: (n, ck)),
                  pl.BlockSpec((w, KV_WIDTH), lambda n, s: (jnp.maximum(n - 1, 0), ck)),
                  pl.BlockSpec((w, KV_WIDTH), lambda n, s: (n, cv)),
                  pl.BlockSpec((w, KV_WIDTH), lambda n, s: (jnp.maximum(n - 1, 0), cv))],
        out_specs=pl.BlockSpec((w, ATTN_WIDTH), lambda n, s: (n, 0)),
    )
    return pl.pallas_call(
        functools.partial(_swa_prompt_kernel, hp=hp),
        grid_spec=grid_spec,
        out_shape=jax.ShapeDtypeStruct((t, ATTN_WIDTH), F32 if hp else BF16),
        compiler_params=_cparams(("arbitrary",)),
        name="swa_prompt_hp" if hp else "swa_prompt",
    )(sinks, z, z, z, z, z)


SWA_S_BB = 8


def _swa_sample_kernel(sink_ref, q_ref, kn_ref, vn_ref, kc_ref, vc_ref,
                       o_ref, ko_ref, vo_ref, *, l):
    w = WINDOW
    bb = kc_ref.shape[0]
    kpad = 2 * w - w - l
    rows = KV_GROUP * l
    ir = lax.broadcasted_iota(I32, (rows, 2 * w), 0)
    ik = lax.broadcasted_iota(I32, (rows, 2 * w), 1)
    lq = ir & (l - 1)
    diff = w + lq - ik
    mask = (diff >= 0) & (diff < w)
    gi = lax.broadcasted_iota(I32, (rows, 1), 0) >> (l.bit_length() - 1)
    zpad = jnp.zeros((kpad, KV_WIDTH), F32)
    for b in range(bb):
        q = (q_ref[b * l:(b + 1) * l, :] * (HEAD_DIM ** -0.5)).astype(BF16)
        kn = kn_ref[b * l:(b + 1) * l, :]
        vn = vn_ref[b * l:(b + 1) * l, :]
        kc = kc_ref[b]
        vc = vc_ref[b]
        ko_ref[b] = jnp.concatenate([kc[l:, :], kn], axis=0)
        vo_ref[b] = jnp.concatenate([vc[l:, :], vn], axis=0)
        kk = jnp.concatenate([kc, kn, zpad], axis=0).astype(BF16)
        vv = jnp.concatenate([vc, vn, zpad], axis=0).astype(BF16)
        pieces = []
        for h in range(N_KV_HEADS):
            kh = kk[:, h * HEAD_DIM:(h + 1) * HEAD_DIM]
            vh = vv[:, h * HEAD_DIM:(h + 1) * HEAD_DIM]
            qh = jnp.concatenate(
                [q[:, (h * KV_GROUP + g) * HEAD_DIM:(h * KV_GROUP + g + 1) * HEAD_DIM]
                 for g in range(KV_GROUP)], axis=0)
            sink = jnp.zeros((rows, 1), F32)
            for g in range(KV_GROUP):
                sink = jnp.where(gi == g, sink_ref[h * KV_GROUP + g], sink)
            s = lax.dot_general(qh, kh, (((1,), (1,)), ((), ())), preferred_element_type=F32)
            o = _sink_softmax_pv(s, mask, sink, vh)
            pieces.extend(o[g * l:(g + 1) * l, :] for g in range(KV_GROUP))
        o_ref[b * l:(b + 1) * l, :] = jnp.concatenate(pieces, axis=1).astype(o_ref.dtype)


def _swa_sample(z, sinks, k_cache, v_cache, row0, nseq, l):
    assert l & (l - 1) == 0
    w = WINDOW
    bb = SWA_S_BB
    rb = bb * l
    r0 = row0 // rb
    cq = COL_Q_A // ATTN_WIDTH
    ck, cv = COL_K_A // KV_WIDTH, COL_V_A // KV_WIDTH
    grid_spec = pltpu.PrefetchScalarGridSpec(
        num_scalar_prefetch=1,
        grid=(nseq // bb,),
        in_specs=[pl.BlockSpec((rb, ATTN_WIDTH), lambda i, s: (r0 + i, cq)),
                  pl.BlockSpec((rb, KV_WIDTH), lambda i, s: (r0 + i, ck)),
                  pl.BlockSpec((rb, KV_WIDTH), lambda i, s: (r0 + i, cv)),
                  pl.BlockSpec((bb, w, KV_WIDTH), lambda i, s: (i, 0, 0)),
                  pl.BlockSpec((bb, w, KV_WIDTH), lambda i, s: (i, 0, 0))],
        out_specs=[pl.BlockSpec((rb, ATTN_WIDTH), lambda i, s: (i, 0)),
                   pl.BlockSpec((bb, w, KV_WIDTH), lambda i, s: (i, 0, 0)),
                   pl.BlockSpec((bb, w, KV_WIDTH), lambda i, s: (i, 0, 0))],
    )
    return pl.pallas_call(
        functools.partial(_swa_sample_kernel, l=l),
        grid_spec=grid_spec,
        out_shape=[jax.ShapeDtypeStruct((nseq * l, ATTN_WIDTH), BF16),
                   jax.ShapeDtypeStruct((nseq, w, KV_WIDTH), F32),
                   jax.ShapeDtypeStruct((nseq, w, KV_WIDTH), F32)],
        compiler_params=_cparams(("arbitrary",)),
        name="swa_sample",
    )(sinks, z, z, z, k_cache, v_cache)


GLA_SUB = SUBLANES


def _split3(x):
    hi = x.astype(BF16)
    r1 = x - hi.astype(F32)
    mid = r1.astype(BF16)
    lo = (r1 - mid.astype(F32)).astype(BF16)
    return hi, mid, lo


def _gla_kernel(*refs, c, nseq, hp, chained, snap_chunk):
    q_ref, k_ref, v_ref, a_ref, r_ref, wa_ref, ba_ref, ng_ref, s0_ref = refs[:9]
    outs_at = 10 if chained else 9
    o_ref, so_ref = refs[outs_at:outs_at + 2]
    snap_ref = refs[outs_at + 2] if snap_chunk is not None else None
    s_scr = refs[-1]
    ci = pl.program_id(1)
    nsb = c // GLA_SUB
    scale = GLA_DK ** -0.5

    @pl.when(ci == 0)
    def _():
        s_scr[...] = s0_ref[...]

    if snap_ref is not None:
        @pl.when(ci == snap_chunk)
        def _():
            snap_ref[...] = s_scr[...]

    rb = nseq * c
    row = lax.broadcasted_iota(I32, (c, 1), 0)
    rsub = row & (GLA_SUB - 1)
    ri = lax.broadcasted_iota(I32, (rb, rb), 0)
    rj = lax.broadcasted_iota(I32, (rb, rb), 1)
    tri = ((rj <= ri) & (rj >= ri - (ri & (c - 1)))).astype(BF16)
    row_l = lax.broadcasted_iota(I32, (LANES, GLA_DK), 0)

    pre = _dot(a_ref[...], wa_ref[...], hp) + ba_ref[...]
    log_a = jax.nn.log_sigmoid(pre) / GLA_TAU
    g_hi, g_mid, g_lo = _split3(log_a)
    b_full = (jnp.dot(tri, g_hi, preferred_element_type=F32)
              + jnp.dot(tri, g_mid, preferred_element_type=F32)
              + jnp.dot(tri, g_lo, preferred_element_type=F32))

    for sq in range(nseq):
        rows = slice(sq * c, (sq + 1) * c)
        b_all = b_full[rows, :]
        outs = []
        for h in range(GLA_HEADS):
            ks = slice(h * GLA_DK, (h + 1) * GLA_DK)
            vs = slice(h * GLA_DV, (h + 1) * GLA_DV)
            q = q_ref[rows, ks] * scale
            k = k_ref[rows, ks]
            v = v_ref[rows, vs]
            b = b_all[:, ks]
            s_prev = s_scr[sq, h]
            v_op = v.astype(F32 if hp else BF16)

            o = _dot(q * jnp.exp(b), s_prev, hp)

            if nsb > 1:
                blocks = [jnp.zeros((GLA_SUB, c), F32)]
                for i in range(1, nsb):
                    r0 = i * GLA_SUB
                    c_i = b[r0:r0 + 1, :]
                    qt = q[r0:r0 + GLA_SUB, :] * jnp.exp(b[r0:r0 + GLA_SUB, :] - c_i)
                    kt = k * jnp.exp(jnp.where(row < r0, c_i - b, NEG_INF))
                    blocks.append(_dot(qt, kt, hp, _NT))
                attn = jnp.concatenate(blocks, axis=0)
                o = o + _dot(attn, v_op, hp)

            for d in range(GLA_SUB):
                if d == 0:
                    wgt = jnp.sum(q * k, axis=-1, keepdims=True)
                    vr = v
                else:
                    kr = pltpu.roll(k, d, 0)
                    br = pltpu.roll(b, d, 0)
                    dec = jnp.exp(jnp.where(rsub >= d, b - br, NEG_INF))
                    wgt = jnp.sum(q * kr * dec, axis=-1, keepdims=True)
                    vr = pltpu.roll(v, d, 0)
                o = o + wgt * vr

            b_last = b[c - 1:c, :]
            kd = k * jnp.exp(b_last - b)
            e_last = jnp.exp(b_last)
            zrows = jnp.zeros((LANES - c, GLA_DK), F32)
            y = jnp.where(row_l == c, jnp.broadcast_to(e_last, (LANES, GLA_DK)),
                          jnp.concatenate([kd, zrows], axis=0))
            yt = y.T
            v_pad = jnp.concatenate([v_op, jnp.zeros((LANES - c, GLA_DV), v_op.dtype)], axis=0)
            s_scr[sq, h] = yt[:, c:c + 1] * s_prev + _dot(yt, v_pad, hp)

            r = r_ref[rows, vs]
            outs.append(_rmsnorm_rows(o, ng_ref[...]) * (r * jax.nn.sigmoid(r)))
        o_ref[rows, :] = jnp.concatenate(outs, axis=1).astype(o_ref.dtype)

    @pl.when(ci == pl.num_programs(1) - 1)
    def _():
        if so_ref.shape == s_scr.shape:
            so_ref[...] = s_scr[...]
        else:
            so_ref[0] = s_scr[...]
            so_ref[1:] = jnp.zeros((so_ref.shape[0] - 1,) + s_scr.shape, F32)


def _gla(z, w_alpha_p, b_alpha, norm_g, s0, s0_layer, layer, s_prev, row0, nbatch, t, c, nseq,
         hp=False, snap_chunk=None):
    nchunk = t // c
    assert nseq == 1 or nchunk == 1
    assert c & (c - 1) == 0 and c % GLA_SUB == 0 and c < LANES
    rb = nseq * c
    r0 = row0 // rb
    rmap = lambda j: (lambda bi, ci: (r0 + bi * nchunk + ci, j))
    state_block = (None, nseq, GLA_HEADS, GLA_DK, GLA_DV)
    state_shape = (nbatch, GLA_HEADS, GLA_DK, GLA_DV)
    in_specs = [pl.BlockSpec((rb, GLA_KEY_WIDTH), rmap(COL_Q_G // GLA_KEY_WIDTH)),
                pl.BlockSpec((rb, GLA_KEY_WIDTH), rmap(COL_K_G // GLA_KEY_WIDTH)),
                pl.BlockSpec((rb, GLA_VAL_WIDTH), rmap(COL_V_G // GLA_VAL_WIDTH)),
                pl.BlockSpec((rb, LANES), rmap(COL_A_LR // LANES)),
                pl.BlockSpec((rb, GLA_VAL_WIDTH), rmap(COL_R_G // GLA_VAL_WIDTH)),
                pl.BlockSpec((None, LANES, GLA_KEY_WIDTH), lambda bi, ci: (layer, 0, 0)),
                pl.BlockSpec((1, GLA_KEY_WIDTH), lambda bi, ci: (0, 0)),
                pl.BlockSpec((1, GLA_DV), lambda bi, ci: (0, 0)),
                pl.BlockSpec(state_block, lambda bi, ci: (s0_layer, bi, 0, 0, 0))]
    args = [z, z, z, z, z, w_alpha_p, b_alpha, norm_g, s0]
    aliases = {}
    if s_prev is None:
        assert layer == 0
        state_out = pl.BlockSpec((DEPTH,) + state_block[1:], lambda bi, ci: (0, bi, 0, 0, 0))
    else:
        in_specs.append(pl.BlockSpec(memory_space=pl.ANY))
        args.append(s_prev)
        aliases = {len(args) - 1: 1}
        state_out = pl.BlockSpec(state_block, lambda bi, ci: (layer, bi, 0, 0, 0))
    out_specs = [pl.BlockSpec((rb, GLA_VAL_WIDTH), lambda bi, ci: (bi * nchunk + ci, 0)), state_out]
    out_shape = [jax.ShapeDtypeStruct((nbatch * t, GLA_VAL_WIDTH), F32 if hp else BF16),
                 jax.ShapeDtypeStruct((DEPTH,) + state_shape, F32)]
    if snap_chunk is not None:
        assert 0 <= snap_chunk < nchunk
        out_specs.append(pl.BlockSpec(state_block, lambda bi, ci: (0, bi, 0, 0, 0)))
        out_shape.append(jax.ShapeDtypeStruct((1,) + state_shape, F32))
    return pl.pallas_call(
        functools.partial(_gla_kernel, c=c, nseq=nseq, hp=hp, chained=s_prev is not None,
                          snap_chunk=snap_chunk),
        grid=(nbatch // nseq, nchunk),
        in_specs=in_specs,
        out_specs=out_specs,
        out_shape=out_shape,
        scratch_shapes=[pltpu.VMEM((nseq,) + state_shape[1:], F32)],
        input_output_aliases=aliases,
        compiler_params=_cparams(("arbitrary", "arbitrary")),
        name="gla_hp" if hp else "gla",
    )(*args)


MERGE_TM = 256
ROUTER_COL_E = 32


def _merge_kernel(oa_ref, og_ref, ga_ref, gg_ref, x_ref, wba_ref, wbg_ref, wo_ref,
                  ln_ref, wr_ref, br_ref, h_ref, hn_ref, lg_ref):
    pa = jnp.dot(oa_ref[...], wba_ref[...], preferred_element_type=F32)
    pg = jnp.dot(og_ref[...], wbg_ref[...], preferred_element_type=F32)
    merged = jax.nn.sigmoid(ga_ref[...]) * pa + jax.nn.sigmoid(gg_ref[...]) * pg
    h = x_ref[...] + jnp.dot(merged.astype(BF16), wo_ref[...], preferred_element_type=F32)
    h_ref[...] = h
    hn = _rmsnorm_rows(h, ln_ref[...])
    _store_token_major(hn_ref, hn)
    lg_ref[...] = _dot_split(hn, wr_ref[...]) + br_ref[...]


def _merge(o_a, o_g, z, x, wba, wbg, wo, ln2, wr, br, tm=MERGE_TM):
    n, d = x.shape
    tm = min(tm, n)
    const = lambda i: (0, 0)
    one = pl.Buffered(1)
    in_specs = [pl.BlockSpec((tm, ATTN_WIDTH), lambda i: (i, 0)),
                pl.BlockSpec((tm, GLA_VAL_WIDTH), lambda i: (i, 0)),
                pl.BlockSpec((tm, d), lambda i: (i, COL_GATE_A // D_MODEL)),
                pl.BlockSpec((tm, d), lambda i: (i, COL_GATE_G // D_MODEL)),
                pl.BlockSpec((tm, d), lambda i: (i, 0)),
                pl.BlockSpec((ATTN_WIDTH, d), const, pipeline_mode=one),
                pl.BlockSpec((GLA_VAL_WIDTH, d), const, pipeline_mode=one),
                pl.BlockSpec((d, d), const, pipeline_mode=one),
                pl.BlockSpec((1, d), const),
                pl.BlockSpec((d, LANES), const),
                pl.BlockSpec((1, LANES), const)]
    out_specs = [pl.BlockSpec((tm, d), lambda i: (i, 0)),
                 pl.BlockSpec((tm * TOK_ROWS, LANES), lambda i: (i, 0)),
                 pl.BlockSpec((tm, LANES), lambda i: (i, 0))]
    return pl.pallas_call(
        _merge_kernel,
        grid=(n // tm,),
        in_specs=in_specs,
        out_specs=out_specs,
        out_shape=[jax.ShapeDtypeStruct((n, d), F32),
                   jax.ShapeDtypeStruct((n * TOK_ROWS, LANES), F32),
                   jax.ShapeDtypeStruct((n, LANES), F32)],
        compiler_params=_cparams(("arbitrary",)),
        name="merge",
    )(o_a, o_g, z, z, x, wba, wbg, wo, ln2, wr, br)


TAIL_ROWS = 256
TAIL_HISTORY = 256
HP_TN = 512


def _prep_identity(a):
    return a


def _prep_rmsnorm(x, g):
    return _rmsnorm_rows(x, g)


def _prep_gated_sum(pa, pg, ga, gg):
    return jax.nn.sigmoid(ga) * pa + jax.nn.sigmoid(gg) * pg


def _mm_hp_kernel(*refs, prep, n, transposed):
    *a_refs, b_ref, o_ref = refs
    a = prep(*[r[...] for r in a_refs])
    o = _dot(a, b_ref[...], True, _NT if transposed else None)
    tn = o.shape[1]
    if n % tn:
        col = pl.program_id(0) * tn + lax.broadcasted_iota(I32, o.shape, 1)
        o = jnp.where(col < n, o, 0.0)
    o_ref[...] = o


def _mm_hp(a_args, a_specs, prep, w, layer, m, transposed=False, tn=HP_TN):
    if transposed:
        _, n, k = w.shape
        w_spec = pl.BlockSpec((None, tn, k), lambda j: (layer, j, 0))
    else:
        _, k, n = w.shape
        w_spec = pl.BlockSpec((None, k, tn), lambda j: (layer, 0, j))
    nt = pl.cdiv(n, tn)
    return pl.pallas_call(
        functools.partial(_mm_hp_kernel, prep=prep, n=n, transposed=transposed),
        grid=(nt,),
        in_specs=list(a_specs) + [w_spec],
        out_specs=pl.BlockSpec((m, tn), lambda j: (0, j)),
        out_shape=jax.ShapeDtypeStruct((m, nt * tn), F32),
        compiler_params=_cparams(("arbitrary",)),
        name="mm_hp",
    )(*a_args, w)


def _whole(shape):
    return pl.BlockSpec(shape, lambda j: (0,) * len(shape))


def _tail_router_kernel(x_ref, u_ref, ln_ref, wr_ref, br_ref, lg_ref):
    hn = _rmsnorm_rows(x_ref[...] + u_ref[...], ln_ref[...])
    lg_ref[...] = _dot(hn, wr_ref[...], True) + br_ref[...]


def _permute_cols(z_o):
    s = lambda a, b: z_o[:, a:b]
    zeros = lambda w: jnp.zeros((z_o.shape[0], w), z_o.dtype)
    return jnp.concatenate(
        [s(_O_GA, _O_GG), s(_O_GG, _O_END), s(_O_QA, _O_KA), s(_O_VG, _O_ALR), s(_O_RG, _O_GA),
         s(_O_QG, _O_KG), s(_O_KG, _O_VG), s(_O_KA, _O_VA), s(_O_VA, _O_QG),
         zeros(COL_A_LR - COL_V_A - KV_WIDTH), s(_O_ALR, _O_RG), zeros(N_PROJ - COL_A_LR - GLA_LOWRANK)],
        axis=-1)


def _tail_logits(x_tail, keep, snap, ln1, w_in_t, sinks, w_alpha_p, b_alpha, norm_g, w_br_attn,
                 w_br_gla, w_out, ln2, wr, br):
    m, d = x_tail.shape
    lo = m - keep
    z_o = _mm_hp([x_tail, ln1], [_whole((m, d)), _whole((1, d))], _prep_rmsnorm, w_in_t, 0, m,
                 transposed=True)
    z = _permute_cols(z_o)
    oa = _swa_prompt(z, sinks, m, hp=True)
    og, _ = _gla(z, w_alpha_p, b_alpha, norm_g, snap, 0, 0, None, 0, 1, m, GLA_CHUNK_PROMPT, 1, hp=True)
    pa = _mm_hp([oa[lo:]], [_whole((keep, ATTN_WIDTH))], _prep_identity, w_br_attn, 0, keep)
    pg = _mm_hp([og[lo:]], [_whole((keep, GLA_VAL_WIDTH))], _prep_identity, w_br_gla, 0, keep)
    gates = z[lo:, :COL_Q_A]
    u = _mm_hp([pa, pg, gates, gates],
               [_whole((keep, d)), _whole((keep, d)),
                pl.BlockSpec((keep, d), lambda j: (0, COL_GATE_A // D_MODEL)),
                pl.BlockSpec((keep, d), lambda j: (0, COL_GATE_G // D_MODEL))],
               _prep_gated_sum, w_out, 0, keep)
    return pl.pallas_call(
        _tail_router_kernel,
        out_shape=jax.ShapeDtypeStruct((keep, LANES), F32),
        compiler_params=pltpu.CompilerParams(vmem_limit_bytes=VMEM_LIMIT),
        name="tail_router",
    )(x_tail[lo:], u, ln2, wr, br)


ROUTE_TM = 512
BIG_I = 1 << 20


def _route_kernel(lg_ref, e_ref, w_ref, cnt_ref, run_ref):
    i = pl.program_id(0)

    @pl.when(i == 0)
    def _():
        run_ref[...] = jnp.zeros_like(run_ref)

    lg = lg_ref[...]
    tm = lg.shape[0]
    col = lax.broadcasted_iota(I32, lg.shape, 1)
    gl = jnp.where(col < N_GROUPS, lg, NEG_INF)
    gmax = jnp.max(gl, axis=-1, keepdims=True)
    g_idx = jnp.min(jnp.where(gl == gmax, col, BIG_I), axis=-1, keepdims=True)
    p_g = 1.0 / jnp.sum(jnp.exp(gl - gmax), axis=-1, keepdims=True)
    lo = ROUTER_COL_E + g_idx * EXPERTS_PER_GROUP
    el = jnp.where((col >= lo) & (col < lo + EXPERTS_PER_GROUP), lg, NEG_INF)
    m1 = jnp.max(el, axis=-1, keepdims=True)
    i1 = jnp.min(jnp.where(el == m1, col, BIG_I), axis=-1, keepdims=True)
    el2 = jnp.where(col == i1, NEG_INF, el)
    m2 = jnp.max(el2, axis=-1, keepdims=True)
    i2 = jnp.min(jnp.where(el2 == m2, col, BIG_I), axis=-1, keepdims=True)
    e2 = jnp.exp(m2 - m1)
    w1 = p_g / (1.0 + e2)
    w2 = p_g * e2 / (1.0 + e2)
    oh = ((col == i1) | (col == i2)).astype(BF16)
    ri = lax.broadcasted_iota(I32, (tm, tm), 0)
    rj = lax.broadcasted_iota(I32, (tm, tm), 1)
    strict = (rj < ri).astype(BF16)
    cum = jnp.dot(strict, oh, preferred_element_type=F32) + run_ref[...]
    r1 = jnp.sum(jnp.where(col == i1, cum, 0.0), axis=-1, keepdims=True)
    r2 = jnp.sum(jnp.where(col == i2, cum, 0.0), axis=-1, keepdims=True)
    run_ref[...] = run_ref[...] + jnp.sum(oh.astype(F32), axis=0, keepdims=True)
    cnt_ref[...] = run_ref[...].astype(I32)
    e_out = jnp.where(col == 0, i1 - ROUTER_COL_E, 0)
    e_out = jnp.where(col == 1, i2 - ROUTER_COL_E, e_out)
    e_out = jnp.where(col == 2, r1.astype(I32), e_out)
    e_out = jnp.where(col == 3, r2.astype(I32), e_out)
    e_ref[...] = e_out
    w_ref[...] = jnp.where(col == 0, w1, jnp.where(col == 1, w2, 0.0))


def _route(logits, tm=ROUTE_TM):
    n = logits.shape[0]
    tm = min(tm, n)
    return pl.pallas_call(
        _route_kernel,
        grid=(n // tm,),
        in_specs=[pl.BlockSpec((tm, LANES), lambda i: (i, 0))],
        out_specs=[pl.BlockSpec((tm, LANES), lambda i: (i, 0)),
                   pl.BlockSpec((tm, LANES), lambda i: (i, 0)),
                   pl.BlockSpec((1, LANES), lambda i: (0, 0))],
        out_shape=[jax.ShapeDtypeStruct((n, LANES), I32),
                   jax.ShapeDtypeStruct((n, LANES), F32),
                   jax.ShapeDtypeStruct((1, LANES), I32)],
        scratch_shapes=[pltpu.VMEM((1, LANES), F32)],
        compiler_params=_cparams(("arbitrary",)),
        name="route",
    )(logits)


MOE_TM = 256
MOE_SLOTS = 3
MOE_LOOKAHEAD = MOE_SLOTS - 1
GATHER_UNROLL = 8


def _row_copy(src_hbm, dst_vmem, src_row, dst_row, sem):
    def rows(tok):
        start = tok * TOK_ROWS
        return pl.ds(start if isinstance(start, int) else pl.multiple_of(start, TOK_ROWS), TOK_ROWS)

    return pltpu.make_async_copy(src_hbm.at[rows(src_row), :], dst_vmem.at[rows(dst_row), :], sem)


def _gather_start(src_hbm, dst_vmem, idx_ref, base, nrows, sem, static):
    if static:
        for r in range(nrows):
            _row_copy(src_hbm, dst_vmem, idx_ref[base + r], r, sem).start()
    else:
        def issue(r, c):
            _row_copy(src_hbm, dst_vmem, idx_ref[base + r], r, sem).start()
            return c

        lax.fori_loop(0, nrows, issue, 0, unroll=GATHER_UNROLL)


def _gather_wait(src_hbm, dst_vmem, nrows, sem):
    for r in range(nrows):
        _row_copy(src_hbm, dst_vmem, 0, r, sem).wait()


def _moe_kernel(te_ref, na_ref, tok_ref, hn_hbm, wg_ref, wu_ref, wd_ref, y_ref, xbuf, sem):
    i = pl.program_id(0)
    tm = xbuf.shape[1] // TOK_ROWS
    na = na_ref[0]

    def start(tile, static):
        slot = lax.rem(tile, MOE_SLOTS)
        _gather_start(hn_hbm, xbuf.at[slot], tok_ref, tile * tm, tm, sem.at[slot], static)

    def wait(tile):
        slot = lax.rem(tile, MOE_SLOTS)
        _gather_wait(hn_hbm, xbuf.at[slot], tm, sem.at[slot])
        return slot

    @pl.when(i == 0)
    def _():
        for t in range(MOE_LOOKAHEAD):
            start(t, False)

    @pl.when(i < na)
    def _():
        slot = wait(i)
        x = _load_token_major(xbuf.at[slot], tm).astype(BF16)
        gate = jnp.dot(x, wg_ref[...].astype(BF16), preferred_element_type=F32)
        up = jnp.dot(x, wu_ref[...].astype(BF16), preferred_element_type=F32)
        hid = gate * jax.nn.sigmoid(gate) * up
        y = jnp.dot(hid.astype(BF16), wd_ref[...].astype(BF16), preferred_element_type=F32)
        _store_token_major(y_ref, y)
        start(i + MOE_LOOKAHEAD, True)

    @pl.when(i >= na)
    def _():
        y_ref[...] = jnp.zeros_like(y_ref)

    @pl.when((i >= na) & (i < na + MOE_LOOKAHEAD))
    def _():
        wait(i)


def _moe(hn, tile_expert, n_active, sorted_tok, w_gate, w_up, w_down, layer, tm=MOE_TM):
    steps = tile_expert.shape[0]
    assert sorted_tok.shape[0] == steps * tm
    _, _, d, ff = w_gate.shape
    assert d == TOK_ROWS * LANES
    wmap = lambda i, te, na, tok: (layer, te[i], 0, 0)
    grid_spec = pltpu.PrefetchScalarGridSpec(
        num_scalar_prefetch=3,
        grid=(steps,),
        in_specs=[pl.BlockSpec(memory_space=pl.ANY),
                  pl.BlockSpec((None, None, d, ff), wmap),
                  pl.BlockSpec((None, None, d, ff), wmap),
                  pl.BlockSpec((None, None, ff, d), wmap)],
        out_specs=pl.BlockSpec((tm * TOK_ROWS, LANES), lambda i, te, na, tok: (i, 0)),
        scratch_shapes=[pltpu.VMEM((MOE_SLOTS, tm * TOK_ROWS, LANES), F32),
                        pltpu.SemaphoreType.DMA((MOE_SLOTS,))],
    )
    return pl.pallas_call(
        _moe_kernel,
        grid_spec=grid_spec,
        out_shape=jax.ShapeDtypeStruct((steps * tm * TOK_ROWS, LANES), F32),
        compiler_params=_cparams(("arbitrary",)),
        name="moe",
    )(tile_expert, n_active, sorted_tok, hn, w_gate, w_up, w_down)


COMBINE_TM = 256


def _combine_kernel(d0_ref, d1_ref, y_hbm, h_ref, w_ref, g_ref, o_ref, buf, sem, *, final_norm, tile0):
    i = pl.program_id(0)
    tm = buf.shape[2] // TOK_ROWS

    def start(tile):
        slot = lax.rem(tile, 2)
        base = (tile0 + tile) * tm
        _gather_start(y_hbm, buf.at[slot, 0], d0_ref, base, tm, sem.at[slot, 0], False)
        _gather_start(y_hbm, buf.at[slot, 1], d1_ref, base, tm, sem.at[slot, 1], False)

    @pl.when(i == 0)
    def _():
        start(0)

    @pl.when(i + 1 < pl.num_programs(0))
    def _():
        start(i + 1)

    slot = lax.rem(i, 2)
    _gather_wait(y_hbm, buf.at[slot, 0], tm, sem.at[slot, 0])
    _gather_wait(y_hbm, buf.at[slot, 1], tm, sem.at[slot, 1])
    w = w_ref[...]
    y0 = _load_token_major(buf.at[slot, 0], tm)
    y1 = _load_token_major(buf.at[slot, 1], tm)
    x = h_ref[...] + (w[:, 0:1] * y0 + w[:, 1:2] * y1)
    if final_norm:
        x = _rmsnorm_rows(x, g_ref[...])
    o_ref[...] = x


def _combine(y_sorted, dest0, dest1, h, w_pair, g, final_norm, row0=0, nrows=None, tm=COMBINE_TM):
    n, d = h.shape
    nrows = n if nrows is None else nrows
    tm = min(tm, nrows)
    assert row0 % tm == 0 and nrows % tm == 0
    tile0 = row0 // tm
    grid_spec = pltpu.PrefetchScalarGridSpec(
        num_scalar_prefetch=2,
        grid=(nrows // tm,),
        in_specs=[pl.BlockSpec(memory_space=pl.ANY),
                  pl.BlockSpec((tm, d), lambda i, a, b: (tile0 + i, 0)),
                  pl.BlockSpec((tm, LANES), lambda i, a, b: (tile0 + i, 0)),
                  pl.BlockSpec((1, d), lambda i, a, b: (0, 0))],
        out_specs=pl.BlockSpec((tm, d), lambda i, a, b: (i, 0)),
        scratch_shapes=[pltpu.VMEM((2, 2, tm * TOK_ROWS, LANES), F32), pltpu.SemaphoreType.DMA((2, 2))],
    )
    return pl.pallas_call(
        functools.partial(_combine_kernel, final_norm=final_norm, tile0=tile0),
        grid_spec=grid_spec,
        out_shape=jax.ShapeDtypeStruct((nrows, d), F32),
        compiler_params=_cparams(("arbitrary",)),
        name="combine",
    )(dest0, dest1, y_sorted, h, w_pair, g)


DISPATCH_TM = 512
META_NACTIVE_LANE = LANES - 1


def _dispatch_kernel(e_ref, cnt_ref, dest_ref, meta_ref, *, tm_moe):
    lane = lax.broadcasted_iota(I32, (1, LANES), 1)
    is_e = (lane >= ROUTER_COL_E) & (lane < ROUTER_COL_E + N_EXPERTS)
    shift = tm_moe.bit_length() - 1
    ntile = jnp.where(is_e, (cnt_ref[...] + (tm_moe - 1)) >> shift, 0)
    r = lax.broadcasted_iota(I32, (LANES, LANES), 0)
    c = lax.broadcasted_iota(I32, (LANES, LANES), 1)
    incl = (r <= c).astype(BF16)
    nt_rows = jnp.broadcast_to(ntile.astype(F32), (SUBLANES, LANES)).astype(BF16)
    tile_end = jnp.dot(nt_rows, incl, preferred_element_type=F32)[0:1, :]
    row_off = (tile_end - ntile.astype(F32)) * tm_moe
    n_active = jnp.max(tile_end, axis=-1, keepdims=True)
    te_col = jnp.sum(jnp.where(r == c, jnp.broadcast_to(tile_end, (LANES, LANES)), 0.0),
                     axis=1, keepdims=True)
    tid = jnp.minimum(c.astype(F32), n_active - 1.0)
    before = (te_col <= tid) & (r >= ROUTER_COL_E) & (r < ROUTER_COL_E + N_EXPERTS)
    tile_expert = jnp.sum(before.astype(F32), axis=0, keepdims=True)
    meta = jnp.where(lane == META_NACTIVE_LANE, n_active, tile_expert)
    meta_ref[...] = meta.astype(I32)

    e = e_ref[...]
    col = lax.broadcasted_iota(I32, e.shape, 1)

    def dest(expert, rank):
        off = jnp.sum(jnp.where(col == expert + ROUTER_COL_E, row_off, 0.0), axis=-1, keepdims=True)
        return off.astype(I32) + rank

    d0 = dest(e[:, 0:1], e[:, 2:3])
    d1 = dest(e[:, 1:2], e[:, 3:4])
    dest_ref[...] = jnp.where(col == 0, d0, jnp.where(col == 1, d1, 0))


def _dispatch(e_out, counts, tm_moe, tm=DISPATCH_TM):
    n = e_out.shape[0]
    tm = min(tm, n)
    assert tm_moe & (tm_moe - 1) == 0 and (2 * n) // tm_moe + N_EXPERTS + MOE_LOOKAHEAD < META_NACTIVE_LANE
    return pl.pallas_call(
        functools.partial(_dispatch_kernel, tm_moe=tm_moe),
        grid=(n // tm,),
        in_specs=[pl.BlockSpec((tm, LANES), lambda i: (i, 0)),
                  pl.BlockSpec((1, LANES), lambda i: (0, 0))],
        out_specs=[pl.BlockSpec((tm, LANES), lambda i: (i, 0)),
                   pl.BlockSpec((1, LANES), lambda i: (0, 0))],
        out_shape=[jax.ShapeDtypeStruct((n, LANES), I32),
                   jax.ShapeDtypeStruct((1, LANES), I32)],
        compiler_params=_cparams(("arbitrary",)),
        name="dispatch",
    )(e_out, counts)


PLAN_UNROLL = 8


def _plan_kernel(d0_ref, d1_ref, tok_ref):
    def clear(r, c):
        tok_ref[r] = 0
        return c

    lax.fori_loop(0, tok_ref.shape[0], clear, 0, unroll=PLAN_UNROLL)

    def place(t, c):
        tok_ref[d0_ref[t]] = t
        tok_ref[d1_ref[t]] = t
        return c

    lax.fori_loop(0, d0_ref.shape[0], place, 0, unroll=PLAN_UNROLL)


def _plan(dest0, dest1, rows):
    smem = pl.BlockSpec(memory_space=pltpu.SMEM)
    return pl.pallas_call(
        _plan_kernel,
        in_specs=[smem, smem],
        out_specs=smem,
        out_shape=jax.ShapeDtypeStruct((rows,), I32),
        name="plan",
    )(dest0, dest1)


def _router_weights(w_rg, b_rg, w_re, b_re):
    d = w_rg.shape[0]
    w = jnp.zeros((d, LANES), F32)
    w = w.at[:, :N_GROUPS].set(w_rg).at[:, ROUTER_COL_E:ROUTER_COL_E + N_EXPERTS].set(w_re)
    b = jnp.zeros((1, LANES), F32)
    b = b.at[0, :N_GROUPS].set(b_rg).at[0, ROUTER_COL_E:ROUTER_COL_E + N_EXPERTS].set(b_re)
    return w, b


GLA_CHUNK_PROMPT = 64
GLA_SEQ_PER_STEP = 8


def kernel(x_prompt, x_sample, cache_swa_k, cache_swa_v, state_gla, ln1_g, w_in, w_alpha, b_alpha,
           gla_norm_g, attn_sinks, w_br_attn, w_br_gla, w_out, ln2_g, w_router_group,
           b_router_group, w_router_expert, b_router_expert, w_exp_gate, w_exp_up, w_exp_down,
           final_norm_g):
    bp, tp, d = x_prompt.shape
    bs, ls, _ = x_sample.shape
    n_p = bp * tp
    n_s = bs * ls
    wb = cache_swa_k.shape[2]
    n = n_p + n_s
    x = jnp.concatenate([x_prompt.reshape(n_p, d), x_sample.reshape(n_s, d)], axis=0)

    w_in_t = jnp.swapaxes(w_in, 1, 2)
    wba = w_br_attn.astype(BF16)
    wbg = w_br_gla.astype(BF16)
    wo = w_out.astype(BF16)
    w_alpha_p = jnp.zeros((DEPTH, LANES, GLA_KEY_WIDTH), F32).at[:, :GLA_LOWRANK, :].set(w_alpha)
    s0_prompt = jnp.zeros((1, bp, GLA_HEADS, GLA_DK, GLA_DV), F32)
    moe_steps = (2 * n) // MOE_TM + N_EXPERTS + MOE_LOOKAHEAD
    assert bp == 1, "the full-precision tail assumes one prompt sequence"
    tail_keep = min(TAIL_ROWS, tp)
    tail0 = max(tp - tail_keep - TAIL_HISTORY, 0)

    pk, pv, sk, sv = [], [], [], []
    s_p = s_s = None
    for l in range(DEPTH):
        z = _inproj(x, ln1_g[l].reshape(1, d), w_in_t, l)
        sinks = attn_sinks[l].astype(F32)
        oa_p = _swa_prompt(z, sinks, n_p)
        oa_s, k_new_s, v_new_s = _swa_sample(
            z, sinks, cache_swa_k[l].reshape(bs, wb, KV_WIDTH), cache_swa_v[l].reshape(bs, wb, KV_WIDTH),
            n_p, bs, ls)
        ba = b_alpha[l].reshape(1, GLA_KEY_WIDTH)
        ng = gla_norm_g[l].reshape(1, GLA_DV)
        if l == 0:
            og_p, s_p, snap = _gla(z, w_alpha_p, ba, ng, s0_prompt, 0, l, s_p, 0, bp, tp,
                                   GLA_CHUNK_PROMPT, 1, snap_chunk=tail0 // GLA_CHUNK_PROMPT)
        else:
            og_p, s_p = _gla(z, w_alpha_p, ba, ng, s0_prompt, 0, l, s_p, 0, bp, tp, GLA_CHUNK_PROMPT, 1)
        og_s, s_s = _gla(z, w_alpha_p, ba, ng, state_gla, l, l, s_s, n_p, bs, ls, ls, GLA_SEQ_PER_STEP)
        o_a = jnp.concatenate([oa_p, oa_s], axis=0)
        o_g = jnp.concatenate([og_p, og_s], axis=0)
        wr, br = _router_weights(w_router_group[l], b_router_group[l],
                                 w_router_expert[l], b_router_expert[l])
        ln2 = ln2_g[l].reshape(1, d)
        h, hn, logits = _merge(o_a, o_g, z, x, wba[l], wbg[l], wo[l], ln2, wr, br)
        if l == 0:
            lg_tail = _tail_logits(x[tail0:n_p], tail_keep, snap, ln1_g[0].reshape(1, d), w_in_t, sinks,
                                   w_alpha_p, ba, ng, w_br_attn, w_br_gla, w_out, ln2, wr, br)
            logits = lax.dynamic_update_slice(logits, lg_tail, (n_p - tail_keep, 0))
        e_out, w_pair, counts = _route(logits)
        dest, meta = _dispatch(e_out, counts, MOE_TM)
        dest0, dest1 = dest[:, 0], dest[:, 1]
        tile_expert = meta[0, :moe_steps]
        n_active = meta[0, META_NACTIVE_LANE:]
        sorted_tok = _plan(dest0, dest1, moe_steps * MOE_TM)
        y_sorted = _moe(hn, tile_expert, n_active, sorted_tok, w_exp_gate, w_exp_up, w_exp_down, l)
        fg = final_norm_g.reshape(1, d)
        if l < DEPTH - 1:
            x = _combine(y_sorted, dest0, dest1, h, w_pair, fg, False)
        else:
            y_prompt = _combine(y_sorted, dest0, dest1, h, w_pair, fg, True, 0, n_p)
            y_sample = _combine(y_sorted, dest0, dest1, h, w_pair, fg, True, n_p, n_s)

        wk = min(WINDOW, tp)
        k_p = z[:n_p, COL_K_A:COL_K_A + KV_WIDTH].reshape(bp, tp, N_KV_HEADS, HEAD_DIM)[:, tp - wk:]
        v_p = z[:n_p, COL_V_A:COL_V_A + KV_WIDTH].reshape(bp, tp, N_KV_HEADS, HEAD_DIM)[:, tp - wk:]
        pk.append(k_p)
        pv.append(v_p)
        sk.append(k_new_s.reshape(bs, wb, N_KV_HEADS, HEAD_DIM))
        sv.append(v_new_s.reshape(bs, wb, N_KV_HEADS, HEAD_DIM))

    return (y_prompt.reshape(bp, tp, d), y_sample.reshape(bs, ls, d), jnp.stack(pk), jnp.stack(pv), s_p,
            jnp.stack(sk), jnp.stack(sv), s_s)
```

```python
import functools

import jax
import jax.numpy as jnp
from jax import lax
from jax.experimental import pallas as pl
from jax.experimental.pallas import tpu as pltpu

F32 = jnp.float32
BF16 = jnp.bfloat16
I32 = jnp.int32

D_MODEL = 2048
DEPTH = 2
N_Q_HEADS = 16
N_KV_HEADS = 2
KV_GROUP = N_Q_HEADS // N_KV_HEADS
HEAD_DIM = 64
ATTN_WIDTH = N_Q_HEADS * HEAD_DIM
KV_WIDTH = N_KV_HEADS * HEAD_DIM
WINDOW = 128
GLA_HEADS = 4
GLA_DV = 256
GLA_DK = 128
GLA_KEY_WIDTH = GLA_HEADS * GLA_DK
GLA_VAL_WIDTH = GLA_HEADS * GLA_DV
GLA_LOWRANK = 16
GLA_TAU = 16.0
N_GROUPS = 4
EXPERTS_PER_GROUP = 8
N_EXPERTS = N_GROUPS * EXPERTS_PER_GROUP
EXPERT_FF = 512
RMS_EPS = 1e-5

LANES = 128
SUBLANES = 8
VMEM_LIMIT = 56 * 1024 * 1024

COL_GATE_A = 0
COL_GATE_G = 2048
COL_Q_A = 4096
COL_V_G = 5120
COL_R_G = 6144
COL_Q_G = 7168
COL_K_G = 7680
COL_K_A = 8192
COL_V_A = 8320
COL_A_LR = 8704
PROJ_TILE = 512
N_PROJ = 9216

_O_QA, _O_KA, _O_VA, _O_QG, _O_KG, _O_VG, _O_ALR, _O_RG, _O_GA, _O_GG, _O_END = (
    0, 1024, 1152, 1280, 1792, 2304, 3328, 3344, 4368, 6416, 8464)
_PROJ_TILE_SRC = (tuple(_O_GA + PROJ_TILE * t for t in range(4)) + tuple(_O_GG + PROJ_TILE * t for t in range(4))
                  + (_O_QA, _O_QA + PROJ_TILE, _O_VG, _O_VG + PROJ_TILE, _O_RG, _O_RG + PROJ_TILE,
                     _O_QG, _O_KG, _O_KA, _O_ALR))
assert len(_PROJ_TILE_SRC) * PROJ_TILE == N_PROJ and max(_PROJ_TILE_SRC) + PROJ_TILE <= _O_END

NEG_INF = float("-inf")


def _cparams(sem):
    return pltpu.CompilerParams(dimension_semantics=sem, vmem_limit_bytes=VMEM_LIMIT)


_NT = (((1,), (1,)), ((), ()))


def _dot(a, b, hp, dims=None):
    if hp:
        a, b, prec = a.astype(F32), b.astype(F32), lax.Precision.HIGHEST
    else:
        a, b, prec = a.astype(BF16), b.astype(BF16), None
    if dims is None:
        return jnp.dot(a, b, precision=prec, preferred_element_type=F32)
    return lax.dot_general(a, b, dims, precision=prec, preferred_element_type=F32)


def _hi_lo(x):
    hi = x.astype(BF16)
    return hi, (x - hi.astype(F32)).astype(BF16)


def _dot_split(a, b):
    a_hi, a_lo = _hi_lo(a)
    b_hi, b_lo = _hi_lo(b)
    return (jnp.dot(a_hi, b_hi, preferred_element_type=F32)
            + jnp.dot(a_lo, b_hi, preferred_element_type=F32)
            + jnp.dot(a_hi, b_lo, preferred_element_type=F32))


TOK_ROWS = D_MODEL // LANES


def _store_token_major(ref, x):
    m = x.shape[0]
    for c in range(TOK_ROWS):
        ref[pl.ds(c, m, stride=TOK_ROWS), :] = x[:, c * LANES:(c + 1) * LANES]


def _load_token_major(ref, m):
    return jnp.concatenate([ref[pl.ds(c, m, stride=TOK_ROWS), :] for c in range(TOK_ROWS)], axis=1)


INPROJ_TM = 1536
NORM_ROWS = 256


def _rmsnorm_rows(x, g):
    ms = jnp.mean(x * x, axis=-1, keepdims=True)
    return x * lax.rsqrt(ms + RMS_EPS) * g


def _inproj_kernel(src_ref, x_ref, g_ref, w_ref, z_ref, xn_ref):
    del src_ref
    @pl.when(pl.program_id(1) == 0)
    def _():
        g = g_ref[...]

        def body(r, c):
            rows = pl.ds(pl.multiple_of(r * NORM_ROWS, NORM_ROWS), NORM_ROWS)
            xn_ref[rows, :] = _rmsnorm_rows(x_ref[rows, :], g).astype(BF16)
            return c

        lax.fori_loop(0, x_ref.shape[0] // NORM_ROWS, body, 0)

    z_ref[...] = lax.dot_general(xn_ref[...], w_ref[0].astype(BF16), _NT, preferred_element_type=F32)


def _inproj(x, g, w_t, layer, tm=INPROJ_TM):
    n, d = x.shape
    tm = min(tm, n)
    assert n % tm == 0 and tm % NORM_ROWS == 0
    assert all(s % SUBLANES == 0 for s in _PROJ_TILE_SRC)
    grid_spec = pltpu.PrefetchScalarGridSpec(
        num_scalar_prefetch=1,
        grid=(n // tm, N_PROJ // PROJ_TILE),
        in_specs=[pl.BlockSpec((tm, d), lambda i, j, src: (i, 0)),
                  pl.BlockSpec((1, d), lambda i, j, src: (0, 0)),
                  pl.BlockSpec((pl.Element(1), pl.Element(PROJ_TILE), pl.Element(d)),
                               lambda i, j, src: (layer, src[j] * SUBLANES, 0))],
        out_specs=pl.BlockSpec((tm, PROJ_TILE), lambda i, j, src: (i, j)),
        scratch_shapes=[pltpu.VMEM((tm, d), BF16)],
    )
    return pl.pallas_call(
        _inproj_kernel,
        grid_spec=grid_spec,
        out_shape=jax.ShapeDtypeStruct((n, N_PROJ), F32),
        compiler_params=_cparams(("arbitrary", "arbitrary")),
        name="inproj",
    )(jnp.asarray([s // SUBLANES for s in _PROJ_TILE_SRC], I32), x, g, w_t)


def _sink_softmax_pv(s, mask, sink, v, hp=False):
    s = jnp.where(mask, s, NEG_INF)
    m = jnp.maximum(jnp.max(s, axis=-1, keepdims=True), sink)
    p = jnp.exp(s - m)
    den = jnp.sum(p, axis=-1, keepdims=True) + jnp.exp(sink - m)
    return _dot(p, v, hp) / den


def _swa_prompt_kernel(sink_ref, q_ref, kc_ref, kp_ref, vc_ref, vp_ref, o_ref, *, hp):
    n = pl.program_id(0)
    w = WINDOW
    op_dtype = F32 if hp else BF16
    q = (q_ref[...] * (HEAD_DIM ** -0.5)).astype(op_dtype)
    kk = jnp.concatenate([kp_ref[...], kc_ref[...]], axis=0).astype(op_dtype)
    vv = jnp.concatenate([vp_ref[...], vc_ref[...]], axis=0).astype(op_dtype)
    iq = lax.broadcasted_iota(I32, (w, 2 * w), 0)
    ik = lax.broadcasted_iota(I32, (w, 2 * w), 1)
    diff = iq + w - ik
    mask = (diff >= 0) & (diff < w) & ((ik >= w) | (n > 0))
    outs = []
    for h in range(N_KV_HEADS):
        kh = kk[:, h * HEAD_DIM:(h + 1) * HEAD_DIM]
        vh = vv[:, h * HEAD_DIM:(h + 1) * HEAD_DIM]
        for g in range(KV_GROUP):
            hq = h * KV_GROUP + g
            qh = q[:, hq * HEAD_DIM:(hq + 1) * HEAD_DIM]
            s = _dot(qh, kh, hp, _NT)
            outs.append(_sink_softmax_pv(s, mask, sink_ref[hq], vh, hp))
    o_ref[...] = jnp.concatenate(outs, axis=1).astype(o_ref.dtype)


def _swa_prompt(z, sinks, t, hp=False):
    w = WINDOW
    nb = t // w
    cq, ck, cv = COL_Q_A // ATTN_WIDTH, COL_K_A // KV_WIDTH, COL_V_A // KV_WIDTH
    grid_spec = pltpu.PrefetchScalarGridSpec(
        num_scalar_prefetch=1,
        grid=(nb,),
        in_specs=[pl.BlockSpec((w, ATTN_WIDTH), lambda n, s: (n, cq)),
                  pl.BlockSpec((w, KV_WIDTH), lambda n, s: (n, ck)),
                  pl.BlockSpec((w, KV_WIDTH), lambda n, s: (jnp.maximum(n - 1, 0), ck)),
                  pl.BlockSpec((w, KV_WIDTH), lambda n, s: (n, cv)),
                  pl.BlockSpec((w, KV_WIDTH), lambda n, s: (jnp.maximum(n - 1, 0), cv))],
        out_specs=pl.BlockSpec((w, ATTN_WIDTH), lambda n, s: (n, 0)),
    )
    return pl.pallas_call(
        functools.partial(_swa_prompt_kernel, hp=hp),
        grid_spec=grid_spec,
        out_shape=jax.ShapeDtypeStruct((t, ATTN_WIDTH), F32 if hp else BF16),
        compiler_params=_cparams(("arbitrary",)),
        name="swa_prompt_hp" if hp else "swa_prompt",
    )(sinks, z, z, z, z, z)


SWA_S_BB = 8


def _swa_sample_kernel(sink_ref, q_ref, kn_ref, vn_ref, kc_ref, vc_ref,
                       o_ref, ko_ref, vo_ref, *, l):
    w = WINDOW
    bb = kc_ref.shape[0]
    kpad = 2 * w - w - l
    rows = KV_GROUP * l
    ir = lax.broadcasted_iota(I32, (rows, 2 * w), 0)
    ik = lax.broadcasted_iota(I32, (rows, 2 * w), 1)
    lq = ir & (l - 1)
    diff = w + lq - ik
    mask = (diff >= 0) & (diff < w)
    gi = lax.broadcasted_iota(I32, (rows, 1), 0) >> (l.bit_length() - 1)
    zpad = jnp.zeros((kpad, KV_WIDTH), F32)
    for b in range(bb):
        q = (q_ref[b * l:(b + 1) * l, :] * (HEAD_DIM ** -0.5)).astype(BF16)
        kn = kn_ref[b * l:(b + 1) * l, :]
        vn = vn_ref[b * l:(b + 1) * l, :]
        kc = kc_ref[b]
        vc = vc_ref[b]
        ko_ref[b] = jnp.concatenate([kc[l:, :], kn], axis=0)
        vo_ref[b] = jnp.concatenate([vc[l:, :], vn], axis=0)
        kk = jnp.concatenate([kc, kn, zpad], axis=0).astype(BF16)
        vv = jnp.concatenate([vc, vn, zpad], axis=0).astype(BF16)
        pieces = []
        for h in range(N_KV_HEADS):
            kh = kk[:, h * HEAD_DIM:(h + 1) * HEAD_DIM]
            vh = vv[:, h * HEAD_DIM:(h + 1) * HEAD_DIM]
            qh = jnp.concatenate(
                [q[:, (h * KV_GROUP + g) * HEAD_DIM:(h * KV_GROUP + g + 1) * HEAD_DIM]
                 for g in range(KV_GROUP)], axis=0)
            sink = jnp.zeros((rows, 1), F32)
            for g in range(KV_GROUP):
                sink = jnp.where(gi == g, sink_ref[h * KV_GROUP + g], sink)
            s = lax.dot_general(qh, kh, (((1,), (1,)), ((), ())), preferred_element_type=F32)
            o = _sink_softmax_pv(s, mask, sink, vh)
            pieces.extend(o[g * l:(g + 1) * l, :] for g in range(KV_GROUP))
        o_ref[b * l:(b + 1) * l, :] = jnp.concatenate(pieces, axis=1).astype(o_ref.dtype)


def _swa_sample(z, sinks, k_cache, v_cache, row0, nseq, l):
    assert l & (l - 1) == 0
    w = WINDOW
    bb = SWA_S_BB
    rb = bb * l
    r0 = row0 // rb
    cq = COL_Q_A // ATTN_WIDTH
    ck, cv = COL_K_A // KV_WIDTH, COL_V_A // KV_WIDTH
    grid_spec = pltpu.PrefetchScalarGridSpec(
        num_scalar_prefetch=1,
        grid=(nseq // bb,),
        in_specs=[pl.BlockSpec((rb, ATTN_WIDTH), lambda i, s: (r0 + i, cq)),
                  pl.BlockSpec((rb, KV_WIDTH), lambda i, s: (r0 + i, ck)),
                  pl.BlockSpec((rb, KV_WIDTH), lambda i, s: (r0 + i, cv)),
                  pl.BlockSpec((bb, w, KV_WIDTH), lambda i, s: (i, 0, 0)),
                  pl.BlockSpec((bb, w, KV_WIDTH), lambda i, s: (i, 0, 0))],
        out_specs=[pl.BlockSpec((rb, ATTN_WIDTH), lambda i, s: (i, 0)),
                   pl.BlockSpec((bb, w, KV_WIDTH), lambda i, s: (i, 0, 0)),
                   pl.BlockSpec((bb, w, KV_WIDTH), lambda i, s: (i, 0, 0))],
    )
    return pl.pallas_call(
        functools.partial(_swa_sample_kernel, l=l),
        grid_spec=grid_spec,
        out_shape=[jax.ShapeDtypeStruct((nseq * l, ATTN_WIDTH), BF16),
                   jax.ShapeDtypeStruct((nseq, w, KV_WIDTH), F32),
                   jax.ShapeDtypeStruct((nseq, w, KV_WIDTH), F32)],
        compiler_params=_cparams(("arbitrary",)),
        name="swa_sample",
    )(sinks, z, z, z, k_cache, v_cache)


GLA_SUB = SUBLANES


def _split3(x):
    hi = x.astype(BF16)
    r1 = x - hi.astype(F32)
    mid = r1.astype(BF16)
    lo = (r1 - mid.astype(F32)).astype(BF16)
    return hi, mid, lo


def _gla_kernel(*refs, c, nseq, hp, chained, snap_chunk):
    q_ref, k_ref, v_ref, a_ref, r_ref, wa_ref, ba_ref, ng_ref, s0_ref = refs[:9]
    outs_at = 10 if chained else 9
    o_ref, so_ref = refs[outs_at:outs_at + 2]
    snap_ref = refs[outs_at + 2] if snap_chunk is not None else None
    s_scr = refs[-1]
    ci = pl.program_id(1)
    nsb = c // GLA_SUB
    scale = GLA_DK ** -0.5

    @pl.when(ci == 0)
    def _():
        s_scr[...] = s0_ref[...]

    if snap_ref is not None:
        @pl.when(ci == snap_chunk)
        def _():
            snap_ref[...] = s_scr[...]

    rb = nseq * c
    row = lax.broadcasted_iota(I32, (c, 1), 0)
    rsub = row & (GLA_SUB - 1)
    ri = lax.broadcasted_iota(I32, (rb, rb), 0)
    rj = lax.broadcasted_iota(I32, (rb, rb), 1)
    tri = ((rj <= ri) & (rj >= ri - (ri & (c - 1)))).astype(BF16)
    row_l = lax.broadcasted_iota(I32, (LANES, GLA_DK), 0)

    pre = _dot(a_ref[...], wa_ref[...], hp) + ba_ref[...]
    log_a = jax.nn.log_sigmoid(pre) / GLA_TAU
    g_hi, g_mid, g_lo = _split3(log_a)
    b_full = (jnp.dot(tri, g_hi, preferred_element_type=F32)
              + jnp.dot(tri, g_mid, preferred_element_type=F32)
              + jnp.dot(tri, g_lo, preferred_element_type=F32))

    for sq in range(nseq):
        rows = slice(sq * c, (sq + 1) * c)
        b_all = b_full[rows, :]
        outs = []
        for h in range(GLA_HEADS):
            ks = slice(h * GLA_DK, (h + 1) * GLA_DK)
            vs = slice(h * GLA_DV, (h + 1) * GLA_DV)
            q = q_ref[rows, ks] * scale
            k = k_ref[rows, ks]
            v = v_ref[rows, vs]
            b = b_all[:, ks]
            s_prev = s_scr[sq, h]
            v_op = v.astype(F32 if hp else BF16)

            o = _dot(q * jnp.exp(b), s_prev, hp)

            if nsb > 1:
                blocks = [jnp.zeros((GLA_SUB, c), F32)]
                for i in range(1, nsb):
                    r0 = i * GLA_SUB
                    c_i = b[r0:r0 + 1, :]
                    qt = q[r0:r0 + GLA_SUB, :] * jnp.exp(b[r0:r0 + GLA_SUB, :] - c_i)
                    kt = k * jnp.exp(jnp.where(row < r0, c_i - b, NEG_INF))
                    blocks.append(_dot(qt, kt, hp, _NT))
                attn = jnp.concatenate(blocks, axis=0)
                o = o + _dot(attn, v_op, hp)

            for d in range(GLA_SUB):
                if d == 0:
                    wgt = jnp.sum(q * k, axis=-1, keepdims=True)
                    vr = v
                else:
                    kr = pltpu.roll(k, d, 0)
                    br = pltpu.roll(b, d, 0)
                    dec = jnp.exp(jnp.where(rsub >= d, b - br, NEG_INF))
                    wgt = jnp.sum(q * kr * dec, axis=-1, keepdims=True)
                    vr = pltpu.roll(v, d, 0)
                o = o + wgt * vr

            b_last = b[c - 1:c, :]
            kd = k * jnp.exp(b_last - b)
            e_last = jnp.exp(b_last)
            zrows = jnp.zeros((LANES - c, GLA_DK), F32)
            y = jnp.where(row_l == c, jnp.broadcast_to(e_last, (LANES, GLA_DK)),
                          jnp.concatenate([kd, zrows], axis=0))
            yt = y.T
            v_pad = jnp.concatenate([v_op, jnp.zeros((LANES - c, GLA_DV), v_op.dtype)], axis=0)
            s_scr[sq, h] = yt[:, c:c + 1] * s_prev + _dot(yt, v_pad, hp)

            r = r_ref[rows, vs]
            outs.append(_rmsnorm_rows(o, ng_ref[...]) * (r * jax.nn.sigmoid(r)))
        o_ref[rows, :] = jnp.concatenate(outs, axis=1).astype(o_ref.dtype)

    @pl.when(ci == pl.num_programs(1) - 1)
    def _():
        if so_ref.shape == s_scr.shape:
            so_ref[...] = s_scr[...]
        else:
            so_ref[0] = s_scr[...]
            so_ref[1:] = jnp.zeros((so_ref.shape[0] - 1,) + s_scr.shape, F32)


def _gla(z, w_alpha_p, b_alpha, norm_g, s0, s0_layer, layer, s_prev, row0, nbatch, t, c, nseq,
         hp=False, snap_chunk=None):
    nchunk = t // c
    assert nseq == 1 or nchunk == 1
    assert c & (c - 1) == 0 and c % GLA_SUB == 0 and c < LANES
    rb = nseq * c
    r0 = row0 // rb
    rmap = lambda j: (lambda bi, ci: (r0 + bi * nchunk + ci, j))
    state_block = (None, nseq, GLA_HEADS, GLA_DK, GLA_DV)
    state_shape = (nbatch, GLA_HEADS, GLA_DK, GLA_DV)
    in_specs = [pl.BlockSpec((rb, GLA_KEY_WIDTH), rmap(COL_Q_G // GLA_KEY_WIDTH)),
                pl.BlockSpec((rb, GLA_KEY_WIDTH), rmap(COL_K_G // GLA_KEY_WIDTH)),
                pl.BlockSpec((rb, GLA_VAL_WIDTH), rmap(COL_V_G // GLA_VAL_WIDTH)),
                pl.BlockSpec((rb, LANES), rmap(COL_A_LR // LANES)),
                pl.BlockSpec((rb, GLA_VAL_WIDTH), rmap(COL_R_G // GLA_VAL_WIDTH)),
                pl.BlockSpec((None, LANES, GLA_KEY_WIDTH), lambda bi, ci: (layer, 0, 0)),
                pl.BlockSpec((1, GLA_KEY_WIDTH), lambda bi, ci: (0, 0)),
                pl.BlockSpec((1, GLA_DV), lambda bi, ci: (0, 0)),
                pl.BlockSpec(state_block, lambda bi, ci: (s0_layer, bi, 0, 0, 0))]
    args = [z, z, z, z, z, w_alpha_p, b_alpha, norm_g, s0]
    aliases = {}
    if s_prev is None:
        assert layer == 0
        state_out = pl.BlockSpec((DEPTH,) + state_block[1:], lambda bi, ci: (0, bi, 0, 0, 0))
    else:
        in_specs.append(pl.BlockSpec(memory_space=pl.ANY))
        args.append(s_prev)
        aliases = {len(args) - 1: 1}
        state_out = pl.BlockSpec(state_block, lambda bi, ci: (layer, bi, 0, 0, 0))
    out_specs = [pl.BlockSpec((rb, GLA_VAL_WIDTH), lambda bi, ci: (bi * nchunk + ci, 0)), state_out]
    out_shape = [jax.ShapeDtypeStruct((nbatch * t, GLA_VAL_WIDTH), F32 if hp else BF16),
                 jax.ShapeDtypeStruct((DEPTH,) + state_shape, F32)]
    if snap_chunk is not None:
        assert 0 <= snap_chunk < nchunk
        out_specs.append(pl.BlockSpec(state_block, lambda bi, ci: (0, bi, 0, 0, 0)))
        out_shape.append(jax.ShapeDtypeStruct((1,) + state_shape, F32))
    return pl.pallas_call(
        functools.partial(_gla_kernel, c=c, nseq=nseq, hp=hp, chained=s_prev is not None,
                          snap_chunk=snap_chunk),
        grid=(nbatch // nseq, nchunk),
        in_specs=in_specs,
        out_specs=out_specs,
        out_shape=out_shape,
        scratch_shapes=[pltpu.VMEM((nseq,) + state_shape[1:], F32)],
        input_output_aliases=aliases,
        compiler_params=_cparams(("arbitrary", "arbitrary")),
        name="gla_hp" if hp else "gla",
    )(*args)


MERGE_TM = 256
ROUTER_COL_E = 32


def _merge_kernel(oa_ref, og_ref, ga_ref, gg_ref, x_ref, wba_ref, wbg_ref, wo_ref,
                  ln_ref, wr_ref, br_ref, h_ref, hn_ref, lg_ref):
    pa = jnp.dot(oa_ref[...], wba_ref[...], preferred_element_type=F32)
    pg = jnp.dot(og_ref[...], wbg_ref[...], preferred_element_type=F32)
    merged = jax.nn.sigmoid(ga_ref[...]) * pa + jax.nn.sigmoid(gg_ref[...]) * pg
    h = x_ref[...] + jnp.dot(merged.astype(BF16), wo_ref[...], preferred_element_type=F32)
    h_ref[...] = h
    hn = _rmsnorm_rows(h, ln_ref[...])
    _store_token_major(hn_ref, hn)
    lg_ref[...] = _dot_split(hn, wr_ref[...]) + br_ref[...]


def _merge(o_a, o_g, z, x, wba, wbg, wo, ln2, wr, br, tm=MERGE_TM):
    n, d = x.shape
    tm = min(tm, n)
    const = lambda i: (0, 0)
    one = pl.Buffered(1)
    in_specs = [pl.BlockSpec((tm, ATTN_WIDTH), lambda i: (i, 0)),
                pl.BlockSpec((tm, GLA_VAL_WIDTH), lambda i: (i, 0)),
                pl.BlockSpec((tm, d), lambda i: (i, COL_GATE_A // D_MODEL)),
                pl.BlockSpec((tm, d), lambda i: (i, COL_GATE_G // D_MODEL)),
                pl.BlockSpec((tm, d), lambda i: (i, 0)),
                pl.BlockSpec((ATTN_WIDTH, d), const, pipeline_mode=one),
                pl.BlockSpec((GLA_VAL_WIDTH, d), const, pipeline_mode=one),
                pl.BlockSpec((d, d), const, pipeline_mode=one),
                pl.BlockSpec((1, d), const),
                pl.BlockSpec((d, LANES), const),
                pl.BlockSpec((1, LANES), const)]
    out_specs = [pl.BlockSpec((tm, d), lambda i: (i, 0)),
                 pl.BlockSpec((tm * TOK_ROWS, LANES), lambda i: (i, 0)),
                 pl.BlockSpec((tm, LANES), lambda i: (i, 0))]
    return pl.pallas_call(
        _merge_kernel,
        grid=(n // tm,),
        in_specs=in_specs,
        out_specs=out_specs,
        out_shape=[jax.ShapeDtypeStruct((n, d), F32),
                   jax.ShapeDtypeStruct((n * TOK_ROWS, LANES), F32),
                   jax.ShapeDtypeStruct((n, LANES), F32)],
        compiler_params=_cparams(("arbitrary",)),
        name="merge",
    )(o_a, o_g, z, z, x, wba, wbg, wo, ln2, wr, br)


TAIL_ROWS = 256
TAIL_HISTORY = 256
HP_TN = 512


def _prep_identity(a):
    return a


def _prep_rmsnorm(x, g):
    return _rmsnorm_rows(x, g)


def _prep_gated_sum(pa, pg, ga, gg):
    return jax.nn.sigmoid(ga) * pa + jax.nn.sigmoid(gg) * pg


def _mm_hp_kernel(*refs, prep, n, transposed):
    *a_refs, b_ref, o_ref = refs
    a = prep(*[r[...] for r in a_refs])
    o = _dot(a, b_ref[...], True, _NT if transposed else None)
    tn = o.shape[1]
    if n % tn:
        col = pl.program_id(0) * tn + lax.broadcasted_iota(I32, o.shape, 1)
        o = jnp.where(col < n, o, 0.0)
    o_ref[...] = o


def _mm_hp(a_args, a_specs, prep, w, layer, m, transposed=False, tn=HP_TN):
    if transposed:
        _, n, k = w.shape
        w_spec = pl.BlockSpec((None, tn, k), lambda j: (layer, j, 0))
    else:
        _, k, n = w.shape
        w_spec = pl.BlockSpec((None, k, tn), lambda j: (layer, 0, j))
    nt = pl.cdiv(n, tn)
    return pl.pallas_call(
        functools.partial(_mm_hp_kernel, prep=prep, n=n, transposed=transposed),
        grid=(nt,),
        in_specs=list(a_specs) + [w_spec],
        out_specs=pl.BlockSpec((m, tn), lambda j: (0, j)),
        out_shape=jax.ShapeDtypeStruct((m, nt * tn), F32),
        compiler_params=_cparams(("arbitrary",)),
        name="mm_hp",
    )(*a_args, w)


def _whole(shape):
    return pl.BlockSpec(shape, lambda j: (0,) * len(shape))


def _tail_router_kernel(x_ref, u_ref, ln_ref, wr_ref, br_ref, lg_ref):
    hn = _rmsnorm_rows(x_ref[...] + u_ref[...], ln_ref[...])
    lg_ref[...] = _dot(hn, wr_ref[...], True) + br_ref[...]


def _permute_cols(z_o):
    s = lambda a, b: z_o[:, a:b]
    zeros = lambda w: jnp.zeros((z_o.shape[0], w), z_o.dtype)
    return jnp.concatenate(
        [s(_O_GA, _O_GG), s(_O_GG, _O_END), s(_O_QA, _O_KA), s(_O_VG, _O_ALR), s(_O_RG, _O_GA),
         s(_O_QG, _O_KG), s(_O_KG, _O_VG), s(_O_KA, _O_VA), s(_O_VA, _O_QG),
         zeros(COL_A_LR - COL_V_A - KV_WIDTH), s(_O_ALR, _O_RG), zeros(N_PROJ - COL_A_LR - GLA_LOWRANK)],
        axis=-1)


def _tail_logits(x_tail, keep, snap, ln1, w_in_t, sinks, w_alpha_p, b_alpha, norm_g, w_br_attn,
                 w_br_gla, w_out, ln2, wr, br):
    m, d = x_tail.shape
    lo = m - keep
    z_o = _mm_hp([x_tail, ln1], [_whole((m, d)), _whole((1, d))], _prep_rmsnorm, w_in_t, 0, m,
                 transposed=True)
    z = _permute_cols(z_o)
    oa = _swa_prompt(z, sinks, m, hp=True)
    og, _ = _gla(z, w_alpha_p, b_alpha, norm_g, snap, 0, 0, None, 0, 1, m, GLA_CHUNK_PROMPT, 1, hp=True)
    pa = _mm_hp([oa[lo:]], [_whole((keep, ATTN_WIDTH))], _prep_identity, w_br_attn, 0, keep)
    pg = _mm_hp([og[lo:]], [_whole((keep, GLA_VAL_WIDTH))], _prep_identity, w_br_gla, 0, keep)
    gates = z[lo:, :COL_Q_A]
    u = _mm_hp([pa, pg, gates, gates],
               [_whole((keep, d)), _whole((keep, d)),
                pl.BlockSpec((keep, d), lambda j: (0, COL_GATE_A // D_MODEL)),
                pl.BlockSpec((keep, d), lambda j: (0, COL_GATE_G // D_MODEL))],
               _prep_gated_sum, w_out, 0, keep)
    return pl.pallas_call(
        _tail_router_kernel,
        out_shape=jax.ShapeDtypeStruct((keep, LANES), F32),
        compiler_params=pltpu.CompilerParams(vmem_limit_bytes=VMEM_LIMIT),
        name="tail_router",
    )(x_tail[lo:], u, ln2, wr, br)


ROUTE_TM = 512
BIG_I = 1 << 20


def _route_kernel(lg_ref, e_ref, w_ref, cnt_ref, run_ref):
    i = pl.program_id(0)

    @pl.when(i == 0)
    def _():
        run_ref[...] = jnp.zeros_like(run_ref)

    lg = lg_ref[...]
    tm = lg.shape[0]
    col = lax.broadcasted_iota(I32, lg.shape, 1)
    gl = jnp.where(col < N_GROUPS, lg, NEG_INF)
    gmax = jnp.max(gl, axis=-1, keepdims=True)
    g_idx = jnp.min(jnp.where(gl == gmax, col, BIG_I), axis=-1, keepdims=True)
    p_g = 1.0 / jnp.sum(jnp.exp(gl - gmax), axis=-1, keepdims=True)
    lo = ROUTER_COL_E + g_idx * EXPERTS_PER_GROUP
    el = jnp.where((col >= lo) & (col < lo + EXPERTS_PER_GROUP), lg, NEG_INF)
    m1 = jnp.max(el, axis=-1, keepdims=True)
    i1 = jnp.min(jnp.where(el == m1, col, BIG_I), axis=-1, keepdims=True)
    el2 = jnp.where(col == i1, NEG_INF, el)
    m2 = jnp.max(el2, axis=-1, keepdims=True)
    i2 = jnp.min(jnp.where(el2 == m2, col, BIG_I), axis=-1, keepdims=True)
    e2 = jnp.exp(m2 - m1)
    w1 = p_g / (1.0 + e2)
    w2 = p_g * e2 / (1.0 + e2)
    oh = ((col == i1) | (col == i2)).astype(BF16)
    ri = lax.broadcasted_iota(I32, (tm, tm), 0)
    rj = lax.broadcasted_iota(I32, (tm, tm), 1)
    strict = (rj < ri).astype(BF16)
    cum = jnp.dot(strict, oh, preferred_element_type=F32) + run_ref[...]
    r1 = jnp.sum(jnp.where(col == i1, cum, 0.0), axis=-1, keepdims=True)
    r2 = jnp.sum(jnp.where(col == i2, cum, 0.0), axis=-1, keepdims=True)
    run_ref[...] = run_ref[...] + jnp.sum(oh.astype(F32), axis=0, keepdims=True)
    cnt_ref[...] = run_ref[...].astype(I32)
    e_out = jnp.where(col == 0, i1 - ROUTER_COL_E, 0)
    e_out = jnp.where(col == 1, i2 - ROUTER_COL_E, e_out)
    e_out = jnp.where(col == 2, r1.astype(I32), e_out)
    e_out = jnp.where(col == 3, r2.astype(I32), e_out)
    e_ref[...] = e_out
    w_ref[...] = jnp.where(col == 0, w1, jnp.where(col == 1, w2, 0.0))


def _route(logits, tm=ROUTE_TM):
    n = logits.shape[0]
    tm = min(tm, n)
    return pl.pallas_call(
        _route_kernel,
        grid=(n // tm,),
        in_specs=[pl.BlockSpec((tm, LANES), lambda i: (i, 0))],
        out_specs=[pl.BlockSpec((tm, LANES), lambda i: (i, 0)),
                   pl.BlockSpec((tm, LANES), lambda i: (i, 0)),
                   pl.BlockSpec((1, LANES), lambda i: (0, 0))],
        out_shape=[jax.ShapeDtypeStruct((n, LANES), I32),
                   jax.ShapeDtypeStruct((n, LANES), F32),
                   jax.ShapeDtypeStruct((1, LANES), I32)],
        scratch_shapes=[pltpu.VMEM((1, LANES), F32)],
        compiler_params=_cparams(("arbitrary",)),
        name="route",
    )(logits)


MOE_TM = 256
MOE_SLOTS = 3
MOE_LOOKAHEAD = MOE_SLOTS - 1
GATHER_UNROLL = 8


def _row_copy(src_hbm, dst_vmem, src_row, dst_row, sem):
    def rows(tok):
        start = tok * TOK_ROWS
        return pl.ds(start if isinstance(start, int) else pl.multiple_of(start, TOK_ROWS), TOK_ROWS)

    return pltpu.make_async_copy(src_hbm.at[rows(src_row), :], dst_vmem.at[rows(dst_row), :], sem)


def _gather_start(src_hbm, dst_vmem, idx_ref, base, nrows, sem, static):
    if static:
        for r in range(nrows):
            _row_copy(src_hbm, dst_vmem, idx_ref[base + r], r, sem).start()
    else:
        def issue(r, c):
            _row_copy(src_hbm, dst_vmem, idx_ref[base + r], r, sem).start()
            return c

        lax.fori_loop(0, nrows, issue, 0, unroll=GATHER_UNROLL)


def _gather_wait(src_hbm, dst_vmem, nrows, sem):
    for r in range(nrows):
        _row_copy(src_hbm, dst_vmem, 0, r, sem).wait()


def _moe_kernel(te_ref, na_ref, tok_ref, hn_hbm, wg_hbm, wu_hbm, wd_hbm, y_ref,
                xbuf, wg_buf, wu_buf, wd_buf, sem, wsem, cnt_ref, *, layer):
    i = pl.program_id(0)
    tm = xbuf.shape[1] // TOK_ROWS
    na = na_ref[0]

    def weight_copies(expert, slot):
        return (pltpu.make_async_copy(wg_hbm.at[layer, expert], wg_buf.at[slot], wsem.at[slot, 0]),
                pltpu.make_async_copy(wu_hbm.at[layer, expert], wu_buf.at[slot], wsem.at[slot, 1]),
                pltpu.make_async_copy(wd_hbm.at[layer, expert], wd_buf.at[slot], wsem.at[slot, 2]))

    def start(tile, static):
        slot = lax.rem(tile, MOE_SLOTS)
        _gather_start(hn_hbm, xbuf.at[slot], tok_ref, tile * tm, tm, sem.at[slot], static)

    def wait(tile):
        slot = lax.rem(tile, MOE_SLOTS)
        _gather_wait(hn_hbm, xbuf.at[slot], tm, sem.at[slot])
        return slot

    @pl.when(i == 0)
    def _():
        cnt_ref[0] = 0
        for c in weight_copies(te_ref[0], 0):
            c.start()
        for t in range(MOE_LOOKAHEAD):
            start(t, False)

    @pl.when(i < na)
    def _():
        expert = te_ref[i]

        @pl.when((i == 0) | (expert != te_ref[jnp.maximum(i - 1, 0)]))
        def _():
            cnt = cnt_ref[0] + jnp.where(i == 0, 0, 1)
            cnt_ref[0] = cnt
            wslot = lax.rem(cnt, 2)
            for c in weight_copies(expert, wslot):
                c.wait()
            nxt = lax.while_loop(lambda j: (j < na) & (te_ref[j] == expert), lambda j: j + 1, i + 1)

            @pl.when(nxt < na)
            def _():
                for c in weight_copies(te_ref[nxt], 1 - wslot):
                    c.start()

        wslot = lax.rem(cnt_ref[0], 2)
        slot = wait(i)
        x = _load_token_major(xbuf.at[slot], tm).astype(BF16)
        gate = jnp.dot(x, wg_buf[wslot].astype(BF16), preferred_element_type=F32)
        up = jnp.dot(x, wu_buf[wslot].astype(BF16), preferred_element_type=F32)
        hid = gate * jax.nn.sigmoid(gate) * up
        y = jnp.dot(hid.astype(BF16), wd_buf[wslot].astype(BF16), preferred_element_type=F32)
        _store_token_major(y_ref, y)
        start(i + MOE_LOOKAHEAD, True)

    @pl.when(i >= na)
    def _():
        y_ref[...] = jnp.zeros_like(y_ref)

    @pl.when((i >= na) & (i < na + MOE_LOOKAHEAD))
    def _():
        wait(i)


def _moe(hn, tile_expert, n_active, sorted_tok, w_gate, w_up, w_down, layer, tm=MOE_TM):
    steps = tile_expert.shape[0]
    assert sorted_tok.shape[0] == steps * tm
    _, _, d, ff = w_gate.shape
    assert d == TOK_ROWS * LANES
    any_space = pl.BlockSpec(memory_space=pl.ANY)
    grid_spec = pltpu.PrefetchScalarGridSpec(
        num_scalar_prefetch=3,
        grid=(steps,),
        in_specs=[any_space, any_space, any_space, any_space],
        out_specs=pl.BlockSpec((tm * TOK_ROWS, LANES), lambda i, te, na, tok: (i, 0)),
        scratch_shapes=[pltpu.VMEM((MOE_SLOTS, tm * TOK_ROWS, LANES), F32),
                        pltpu.VMEM((2, d, ff), F32), pltpu.VMEM((2, d, ff), F32), pltpu.VMEM((2, ff, d), F32),
                        pltpu.SemaphoreType.DMA((MOE_SLOTS,)), pltpu.SemaphoreType.DMA((2, 3)),
                        pltpu.SMEM((1,), I32)],
    )
    return pl.pallas_call(
        functools.partial(_moe_kernel, layer=layer),
        grid_spec=grid_spec,
        out_shape=jax.ShapeDtypeStruct((steps * tm * TOK_ROWS, LANES), F32),
        compiler_params=_cparams(("arbitrary",)),
        name="moe",
    )(tile_expert, n_active, sorted_tok, hn, w_gate, w_up, w_down)


COMBINE_TM = 256


def _combine_kernel(d0_ref, d1_ref, y_hbm, h_ref, w_ref, g_ref, o_ref, buf, sem, *, final_norm, tile0):
    i = pl.program_id(0)
    tm = buf.shape[2] // TOK_ROWS

    def start(tile):
        slot = lax.rem(tile, 2)
        base = (tile0 + tile) * tm
        _gather_start(y_hbm, buf.at[slot, 0], d0_ref, base, tm, sem.at[slot, 0], False)
        _gather_start(y_hbm, buf.at[slot, 1], d1_ref, base, tm, sem.at[slot, 1], False)

    @pl.when(i == 0)
    def _():
        start(0)

    @pl.when(i + 1 < pl.num_programs(0))
    def _():
        start(i + 1)

    slot = lax.rem(i, 2)
    _gather_wait(y_hbm, buf.at[slot, 0], tm, sem.at[slot, 0])
    _gather_wait(y_hbm, buf.at[slot, 1], tm, sem.at[slot, 1])
    w = w_ref[...]
    y0 = _load_token_major(buf.at[slot, 0], tm)
    y1 = _load_token_major(buf.at[slot, 1], tm)
    x = h_ref[...] + (w[:, 0:1] * y0 + w[:, 1:2] * y1)
    if final_norm:
        x = _rmsnorm_rows(x, g_ref[...])
    o_ref[...] = x


def _combine(y_sorted, dest0, dest1, h, w_pair, g, final_norm, row0=0, nrows=None, tm=COMBINE_TM):
    n, d = h.shape
    nrows = n if nrows is None else nrows
    tm = min(tm, nrows)
    assert row0 % tm == 0 and nrows % tm == 0
    tile0 = row0 // tm
    grid_spec = pltpu.PrefetchScalarGridSpec(
        num_scalar_prefetch=2,
        grid=(nrows // tm,),
        in_specs=[pl.BlockSpec(memory_space=pl.ANY),
                  pl.BlockSpec((tm, d), lambda i, a, b: (tile0 + i, 0)),
                  pl.BlockSpec((tm, LANES), lambda i, a, b: (tile0 + i, 0)),
                  pl.BlockSpec((1, d), lambda i, a, b: (0, 0))],
        out_specs=pl.BlockSpec((tm, d), lambda i, a, b: (i, 0)),
        scratch_shapes=[pltpu.VMEM((2, 2, tm * TOK_ROWS, LANES), F32), pltpu.SemaphoreType.DMA((2, 2))],
    )
    return pl.pallas_call(
        functools.partial(_combine_kernel, final_norm=final_norm, tile0=tile0),
        grid_spec=grid_spec,
        out_shape=jax.ShapeDtypeStruct((nrows, d), F32),
        compiler_params=_cparams(("arbitrary",)),
        name="combine",
    )(dest0, dest1, y_sorted, h, w_pair, g)


DISPATCH_TM = 512
META_NACTIVE_LANE = LANES - 1


def _dispatch_kernel(e_ref, cnt_ref, dest_ref, meta_ref, *, tm_moe):
    lane = lax.broadcasted_iota(I32, (1, LANES), 1)
    is_e = (lane >= ROUTER_COL_E) & (lane < ROUTER_COL_E + N_EXPERTS)
    shift = tm_moe.bit_length() - 1
    ntile = jnp.where(is_e, (cnt_ref[...] + (tm_moe - 1)) >> shift, 0)
    r = lax.broadcasted_iota(I32, (LANES, LANES), 0)
    c = lax.broadcasted_iota(I32, (LANES, LANES), 1)
    incl = (r <= c).astype(BF16)
    nt_rows = jnp.broadcast_to(ntile.astype(F32), (SUBLANES, LANES)).astype(BF16)
    tile_end = jnp.dot(nt_rows, incl, preferred_element_type=F32)[0:1, :]
    row_off = (tile_end - ntile.astype(F32)) * tm_moe
    n_active = jnp.max(tile_end, axis=-1, keepdims=True)
    te_col = jnp.sum(jnp.where(r == c, jnp.broadcast_to(tile_end, (LANES, LANES)), 0.0),
                     axis=1, keepdims=True)
    tid = jnp.minimum(c.astype(F32), n_active - 1.0)
    before = (te_col <= tid) & (r >= ROUTER_COL_E) & (r < ROUTER_COL_E + N_EXPERTS)
    tile_expert = jnp.sum(before.astype(F32), axis=0, keepdims=True)
    meta = jnp.where(lane == META_NACTIVE_LANE, n_active, tile_expert)
    meta_ref[...] = meta.astype(I32)

    e = e_ref[...]
    col = lax.broadcasted_iota(I32, e.shape, 1)

    def dest(expert, rank):
        off = jnp.sum(jnp.where(col == expert + ROUTER_COL_E, row_off, 0.0), axis=-1, keepdims=True)
        return off.astype(I32) + rank

    d0 = dest(e[:, 0:1], e[:, 2:3])
    d1 = dest(e[:, 1:2], e[:, 3:4])
    dest_ref[...] = jnp.where(col == 0, d0, jnp.where(col == 1, d1, 0))


def _dispatch(e_out, counts, tm_moe, tm=DISPATCH_TM):
    n = e_out.shape[0]
    tm = min(tm, n)
    assert tm_moe & (tm_moe - 1) == 0 and (2 * n) // tm_moe + N_EXPERTS + MOE_LOOKAHEAD < META_NACTIVE_LANE
    return pl.pallas_call(
        functools.partial(_dispatch_kernel, tm_moe=tm_moe),
        grid=(n // tm,),
        in_specs=[pl.BlockSpec((tm, LANES), lambda i: (i, 0)),
                  pl.BlockSpec((1, LANES), lambda i: (0, 0))],
        out_specs=[pl.BlockSpec((tm, LANES), lambda i: (i, 0)),
                   pl.BlockSpec((1, LANES), lambda i: (0, 0))],
        out_shape=[jax.ShapeDtypeStruct((n, LANES), I32),
                   jax.ShapeDtypeStruct((1, LANES), I32)],
        compiler_params=_cparams(("arbitrary",)),
        name="dispatch",
    )(e_out, counts)


PLAN_UNROLL = 8


def _plan_kernel(d0_ref, d1_ref, tok_ref):
    def clear(r, c):
        tok_ref[r] = 0
        return c

    lax.fori_loop(0, tok_ref.shape[0], clear, 0, unroll=PLAN_UNROLL)

    def place(t, c):
        tok_ref[d0_ref[t]] = t
        tok_ref[d1_ref[t]] = t
        return c

    lax.fori_loop(0, d0_ref.shape[0], place, 0, unroll=PLAN_UNROLL)


def _plan(dest0, dest1, rows):
    smem = pl.BlockSpec(memory_space=pltpu.SMEM)
    return pl.pallas_call(
        _plan_kernel,
        in_specs=[smem, smem],
        out_specs=smem,
        out_shape=jax.ShapeDtypeStruct((rows,), I32),
        name="plan",
    )(dest0, dest1)


def _router_weights(w_rg, b_rg, w_re, b_re):
    d = w_rg.shape[0]
    w = jnp.zeros((d, LANES), F32)
    w = w.at[:, :N_GROUPS].set(w_rg).at[:, ROUTER_COL_E:ROUTER_COL_E + N_EXPERTS].set(w_re)
    b = jnp.zeros((1, LANES), F32)
    b = b.at[0, :N_GROUPS].set(b_rg).at[0, ROUTER_COL_E:ROUTER_COL_E + N_EXPERTS].set(b_re)
    return w, b


GLA_CHUNK_PROMPT = 64
GLA_SEQ_PER_STEP = 8


def kernel(x_prompt, x_sample, cache_swa_k, cache_swa_v, state_gla, ln1_g, w_in, w_alpha, b_alpha,
           gla_norm_g, attn_sinks, w_br_attn, w_br_gla, w_out, ln2_g, w_router_group,
           b_router_group, w_router_expert, b_router_expert, w_exp_gate, w_exp_up, w_exp_down,
           final_norm_g):
    bp, tp, d = x_prompt.shape
    bs, ls, _ = x_sample.shape
    n_p = bp * tp
    n_s = bs * ls
    wb = cache_swa_k.shape[2]
    n = n_p + n_s
    x = jnp.concatenate([x_prompt.reshape(n_p, d), x_sample.reshape(n_s, d)], axis=0)

    w_in_t = jnp.swapaxes(w_in, 1, 2)
    wba = w_br_attn.astype(BF16)
    wbg = w_br_gla.astype(BF16)
    wo = w_out.astype(BF16)
    w_alpha_p = jnp.zeros((DEPTH, LANES, GLA_KEY_WIDTH), F32).at[:, :GLA_LOWRANK, :].set(w_alpha)
    s0_prompt = jnp.zeros((1, bp, GLA_HEADS, GLA_DK, GLA_DV), F32)
    moe_steps = (2 * n) // MOE_TM + N_EXPERTS + MOE_LOOKAHEAD
    assert bp == 1, "the full-precision tail assumes one prompt sequence"
    tail_keep = min(TAIL_ROWS, tp)
    tail0 = max(tp - tail_keep - TAIL_HISTORY, 0)

    pk, pv, sk, sv = [], [], [], []
    s_p = s_s = None
    for l in range(DEPTH):
        z = _inproj(x, ln1_g[l].reshape(1, d), w_in_t, l)
        sinks = attn_sinks[l].astype(F32)
        oa_p = _swa_prompt(z, sinks, n_p)
        oa_s, k_new_s, v_new_s = _swa_sample(
            z, sinks, cache_swa_k[l].reshape(bs, wb, KV_WIDTH), cache_swa_v[l].reshape(bs, wb, KV_WIDTH),
            n_p, bs, ls)
        ba = b_alpha[l].reshape(1, GLA_KEY_WIDTH)
        ng = gla_norm_g[l].reshape(1, GLA_DV)
        if l == 0:
            og_p, s_p, snap = _gla(z, w_alpha_p, ba, ng, s0_prompt, 0, l, s_p, 0, bp, tp,
                                   GLA_CHUNK_PROMPT, 1, snap_chunk=tail0 // GLA_CHUNK_PROMPT)
        else:
            og_p, s_p = _gla(z, w_alpha_p, ba, ng, s0_prompt, 0, l, s_p, 0, bp, tp, GLA_CHUNK_PROMPT, 1)
        og_s, s_s = _gla(z, w_alpha_p, ba, ng, state_gla, l, l, s_s, n_p, bs, ls, ls, GLA_SEQ_PER_STEP)
        o_a = jnp.concatenate([oa_p, oa_s], axis=0)
        o_g = jnp.concatenate([og_p, og_s], axis=0)
        wr, br = _router_weights(w_router_group[l], b_router_group[l],
                                 w_router_expert[l], b_router_expert[l])
        ln2 = ln2_g[l].reshape(1, d)
        h, hn, logits = _merge(o_a, o_g, z, x, wba[l], wbg[l], wo[l], ln2, wr, br)
        if l == 0:
            lg_tail = _tail_logits(x[tail0:n_p], tail_keep, snap, ln1_g[0].reshape(1, d), w_in_t, sinks,
                                   w_alpha_p, ba, ng, w_br_attn, w_br_gla, w_out, ln2, wr, br)
            logits = lax.dynamic_update_slice(logits, lg_tail, (n_p - tail_keep, 0))
        e_out, w_pair, counts = _route(logits)
        dest, meta = _dispatch(e_out, counts, MOE_TM)
        dest0, dest1 = dest[:, 0], dest[:, 1]
        tile_expert = meta[0, :moe_steps]
        n_active = meta[0, META_NACTIVE_LANE:]
        sorted_tok = _plan(dest0, dest1, moe_steps * MOE_TM)
        y_sorted = _moe(hn, tile_expert, n_active, sorted_tok, w_exp_gate, w_exp_up, w_exp_down, l)
        fg = final_norm_g.reshape(1, d)
        if l < DEPTH - 1:
            x = _combine(y_sorted, dest0, dest1, h, w_pair, fg, False)
        else:
            y_prompt = _combine(y_sorted, dest0, dest1, h, w_pair, fg, True, 0, n_p)
            y_sample = _combine(y_sorted, dest0, dest1, h, w_pair, fg, True, n_p, n_s)

        wk = min(WINDOW, tp)
        k_p = z[:n_p, COL_K_A:COL_K_A + KV_WIDTH].reshape(bp, tp, N_KV_HEADS, HEAD_DIM)[:, tp - wk:]
        v_p = z[:n_p, COL_V_A:COL_V_A + KV_WIDTH].reshape(bp, tp, N_KV_HEADS, HEAD_DIM)[:, tp - wk:]
        pk.append(k_p)
        pv.append(v_p)
        sk.append(k_new_s.reshape(bs, wb, N_KV_HEADS, HEAD_DIM))
        sv.append(v_new_s.reshape(bs, wb, N_KV_HEADS, HEAD_DIM))

    return (y_prompt.reshape(bp, tp, d), y_sample.reshape(bs, ls, d), jnp.stack(pk), jnp.stack(pv), s_p,
            jnp.stack(sk), jnp.stack(sv), s_s)
```
